```python
import math
import jax
import jax.numpy as jnp
from jax import lax
import numpy as np

D_MODEL = 1024
BATCH = 16
SEQ = 4096
DEPTH = 1
DEC_BATCH = 8
DEC_SEQ = 64
PAST_LEN = 4096

CHUNK = 64
N_META = 16
POOL_WINDOWS = (2, 4, 8, 16)
N_POOL_GROUPS = 4
POOL_WIDTH = D_MODEL // 2
POOL_GROUP = POOL_WIDTH // N_POOL_GROUPS
POOL_STATE = max(POOL_WINDOWS) - 1
SB_HEADS = 8
SB_HEAD_DIM = 64
SB_WIDTH = SB_HEADS * SB_HEAD_DIM
Q_BLOCK = 128
N_GROUPS = 4
EXPERTS_PER_GROUP = 8
N_EXPERTS = N_GROUPS * EXPERTS_PER_GROUP
TOP_K = 2
D_EXPERT = D_MODEL // 2
MOE_BLOCK = 256
NORM_EPS = 1e-6
IN_SPLITS = (POOL_WIDTH, POOL_WIDTH + SB_WIDTH, POOL_WIDTH + 2 * SB_WIDTH,
             POOL_WIDTH + 3 * SB_WIDTH, POOL_WIDTH + 3 * SB_WIDTH + D_MODEL)
IN_WIDTH = POOL_WIDTH + 3 * SB_WIDTH + 2 * D_MODEL

kernel_name = 'hybrid_pool_stickbreak_hmoe_step'


def rms_norm(x, g):
    xf = x.astype(jnp.float32)
    y = xf * lax.rsqrt(jnp.mean(xf * xf, axis=-1, keepdims=True) + NORM_EPS)
    return (y * g.astype(jnp.float32)).astype(x.dtype)


def multiscale_pool(u, prefix, offset, pool_mix, pool_scale):
    b, n, _ = u.shape
    uf = u.astype(jnp.float32)
    ext = jnp.concatenate([prefix.astype(jnp.float32), uf], axis=1)
    cs = jnp.concatenate([jnp.zeros_like(ext[:, :1]), jnp.cumsum(ext, axis=1)], axis=1)
    hi = cs[:, POOL_STATE + 1:POOL_STATE + 1 + n]
    t = jnp.arange(n)
    groups = []
    for gi, w in enumerate(POOL_WINDOWS):
        c0, c1 = gi * POOL_GROUP, (gi + 1) * POOL_GROUP
        lo = cs[:, POOL_STATE + 1 - w:POOL_STATE + 1 - w + n, c0:c1]
        cnt = jnp.minimum(w, t + 1 + offset).astype(jnp.float32)[None, :, None]
        groups.append((hi[..., c0:c1] - lo) / cnt - uf[..., c0:c1])
    pooled = jnp.stack(groups, axis=2)
    mixed = jnp.einsum('bngc,gcd->bngd', pooled, pool_mix.astype(jnp.float32))
    return (mixed.reshape(b, n, POOL_WIDTH) * pool_scale.astype(jnp.float32)).astype(u.dtype)


def stick_breaking_block(q, qpos, k, v, kpos):
    z = jnp.einsum('bqhd,bkhd->bhqk', q, k) * (SB_HEAD_DIM ** -0.5)
    mask = kpos[None, :] < qpos[:, None]
    log_keep = jnp.where(mask, jax.nn.log_sigmoid(-z), 0.0)
    after = lax.cumsum(log_keep, axis=3, reverse=True) - log_keep
    a = jnp.where(mask, jnp.exp(jax.nn.log_sigmoid(z) + after), 0.0)
    return jnp.einsum('bhqk,bkhd->bqhd', a, v)


def stick_breaking_prompt(q, k, v):
    b, n = q.shape[:2]
    nb = -(-n // Q_BLOCK)
    pad = nb * Q_BLOCK - n
    qb = jnp.pad(q.astype(jnp.float32), ((0, 0), (0, pad), (0, 0), (0, 0)))
    qb = qb.reshape(b, nb, Q_BLOCK, SB_HEADS, SB_HEAD_DIM).transpose(1, 0, 2, 3, 4)
    qpos = jnp.arange(nb * Q_BLOCK).reshape(nb, Q_BLOCK)
    kf = k.astype(jnp.float32)
    vf = v.astype(jnp.float32)
    kpos = jnp.arange(n)
    out = lax.map(lambda blk: stick_breaking_block(blk[0], blk[1], kf, vf, kpos), (qb, qpos))
    out = out.transpose(1, 0, 2, 3, 4).reshape(b, nb * Q_BLOCK, SB_HEADS, SB_HEAD_DIM)
    return out[:, :n].astype(q.dtype)


def stick_breaking_step(q, k, v, k_past, v_past):
    s = q.shape[1]
    past = k_past.shape[1]
    k_all = jnp.concatenate([k_past.astype(jnp.float32), k.astype(jnp.float32)], axis=1)
    v_all = jnp.concatenate([v_past.astype(jnp.float32), v.astype(jnp.float32)], axis=1)
    qpos = past + jnp.arange(s)
    kpos = jnp.arange(past + s)
    out = stick_breaking_block(q.astype(jnp.float32), qpos, k_all, v_all, kpos)
    return out.astype(q.dtype)


def hier_route(hf, w_rg, b_rg, w_re, b_re):
    n = hf.shape[0]
    x32 = hf.astype(jnp.float32)
    rows = jnp.arange(n)
    cg = x32 @ w_rg.astype(jnp.float32) + b_rg.astype(jnp.float32)
    g_idx = jnp.argmax(cg, axis=-1)
    p_g = jax.nn.softmax(cg, axis=-1)[rows, g_idx]
    fe = (x32 @ w_re.astype(jnp.float32) + b_re.astype(jnp.float32)).reshape(n, N_GROUPS, EXPERTS_PER_GROUP)
    p_e = jax.nn.softmax(fe[rows, g_idx], axis=-1)
    top_p, top_i = lax.top_k(p_e, TOP_K)
    top_p = top_p / jnp.sum(top_p, axis=-1, keepdims=True)
    experts = g_idx[:, None].astype(jnp.int32) * EXPERTS_PER_GROUP + top_i.astype(jnp.int32)
    return experts, top_p * p_g[:, None]


def hier_moe(h, w_rg, b_rg, w_re, b_re, w_gate, w_up, w_down):
    b, n, d = h.shape
    ntok = b * n
    hf = h.reshape(ntok, d)
    experts, weights = hier_route(hf, w_rg, b_rg, w_re, b_re)
    n_assign = ntok * TOP_K
    e_flat = experts.reshape(n_assign)
    order = jnp.argsort(e_flat)
    e_sorted = e_flat[order]
    counts = jnp.bincount(e_flat, length=N_EXPERTS)
    padded = (counts + MOE_BLOCK - 1) // MOE_BLOCK * MOE_BLOCK
    starts = jnp.cumsum(counts) - counts
    pends = jnp.cumsum(padded)
    pstarts = pends - padded
    dest_sorted = pstarts[e_sorted] + jnp.arange(n_assign) - starts[e_sorted]
    cap = -(-(n_assign + N_EXPERTS * (MOE_BLOCK - 1)) // MOE_BLOCK) * MOE_BLOCK
    nblk = cap // MOE_BLOCK
    buf = jnp.zeros((cap, d), h.dtype).at[dest_sorted].set(hf[order // TOP_K])
    blk_e = jnp.minimum(jnp.searchsorted(pends, jnp.arange(nblk) * MOE_BLOCK, side='right'), N_EXPERTS - 1)

    def expert_block(args):
        xb, e = args
        gate = xb @ w_gate[e]
        up = xb @ w_up[e]
        return (jax.nn.silu(gate) * up) @ w_down[e]

    yb = lax.map(expert_block, (buf.reshape(nblk, MOE_BLOCK, d), blk_e)).reshape(cap, d)
    dest = jnp.zeros((n_assign,), jnp.int32).at[order].set(dest_sorted.astype(jnp.int32))
    y = yb[dest].reshape(ntok, TOP_K, d).astype(jnp.float32)
    out = jnp.einsum('nkd,nk->nd', y, weights)
    return out.astype(h.dtype).reshape(b, n, d)


def trunk_layer(h, pool_prefix, pool_offset, k_past, v_past,
                norm_mix, w_in, q_norm, k_norm, pool_mix, pool_scale,
                w_pool_out, w_sb_out, w_o, norm_ffn,
                w_router_group, b_router_group, w_router_expert, b_router_expert,
                w_gate, w_up, w_down):
    b, n, _ = h.shape
    hn = rms_norm(h, norm_mix)
    z = hn @ w_in
    u, q, k, v, g_pool, g_sb = jnp.split(z, IN_SPLITS, axis=-1)
    q = rms_norm(q.reshape(b, n, SB_HEADS, SB_HEAD_DIM), q_norm)
    k = rms_norm(k.reshape(b, n, SB_HEADS, SB_HEAD_DIM), k_norm)
    v = v.reshape(b, n, SB_HEADS, SB_HEAD_DIM)
    pool_out = multiscale_pool(u, pool_prefix, pool_offset, pool_mix, pool_scale)
    if k_past is None:
        sb_out = stick_breaking_prompt(q, k, v)
    else:
        sb_out = stick_breaking_step(q, k, v, k_past, v_past)
    merged = (jax.nn.sigmoid(g_pool) * (pool_out @ w_pool_out)
              + jax.nn.sigmoid(g_sb) * (sb_out.reshape(b, n, SB_WIDTH) @ w_sb_out))
    h = h + merged @ w_o
    h = h + hier_moe(rms_norm(h, norm_ffn), w_router_group, b_router_group,
                     w_router_expert, b_router_expert, w_gate, w_up, w_down)
    new_pool = jnp.concatenate([pool_prefix.astype(u.dtype), u], axis=1)[:, -POOL_STATE:]
    return h, k, v, new_pool


def setup_inputs(seed: int = 0) -> dict:
    key = jax.random.key(seed)
    ks = jax.random.split(key, 24)

    def nrm(k, shape, scale):
        return jax.random.normal(k, shape, jnp.float32) * scale

    return {
        'x_prompt': nrm(ks[0], (BATCH, SEQ, D_MODEL), 1.0),
        'x_sample': nrm(ks[1], (DEC_BATCH, DEC_SEQ, D_MODEL), 1.0),
        'cache_sb_k': nrm(ks[2], (DEPTH, DEC_BATCH, PAST_LEN, SB_HEADS, SB_HEAD_DIM), 1.0),
        'cache_sb_v': nrm(ks[3], (DEPTH, DEC_BATCH, PAST_LEN, SB_HEADS, SB_HEAD_DIM), 1.0),
        'state_pool': nrm(ks[4], (DEPTH, DEC_BATCH, POOL_STATE, POOL_WIDTH), 1.0),
        'meta_tokens': nrm(ks[5], (N_META, D_MODEL), 1.0),
        'norm_mix': 1.0 + nrm(ks[6], (DEPTH, D_MODEL), 0.02),
        'w_in': nrm(ks[7], (DEPTH, D_MODEL, IN_WIDTH), D_MODEL ** -0.5),
        'q_norm': 1.0 + nrm(ks[8], (DEPTH, SB_HEAD_DIM), 0.02),
        'k_norm': 1.0 + nrm(ks[9], (DEPTH, SB_HEAD_DIM), 0.02),
        'pool_mix': nrm(ks[10], (DEPTH, N_POOL_GROUPS, POOL_GROUP, POOL_GROUP), POOL_GROUP ** -0.5),
        'pool_scale': 1.0 + nrm(ks[11], (DEPTH, POOL_WIDTH), 0.1),
        'w_pool_out': nrm(ks[12], (DEPTH, POOL_WIDTH, D_MODEL), POOL_WIDTH ** -0.5),
        'w_sb_out': nrm(ks[13], (DEPTH, SB_WIDTH, D_MODEL), SB_WIDTH ** -0.5),
        'w_o': nrm(ks[14], (DEPTH, D_MODEL, D_MODEL), D_MODEL ** -0.5),
        'norm_ffn': 1.0 + nrm(ks[15], (DEPTH, D_MODEL), 0.02),
        'w_router_group': nrm(ks[16], (DEPTH, D_MODEL, N_GROUPS), D_MODEL ** -0.5),
        'b_router_group': nrm(ks[17], (DEPTH, N_GROUPS), 0.01),
        'w_router_expert': nrm(ks[18], (DEPTH, D_MODEL, N_EXPERTS), D_MODEL ** -0.5),
        'b_router_expert': nrm(ks[19], (DEPTH, N_EXPERTS), 0.01),
        'w_gate': nrm(ks[20], (DEPTH, N_EXPERTS, D_MODEL, D_EXPERT), D_MODEL ** -0.5),
        'w_up': nrm(ks[21], (DEPTH, N_EXPERTS, D_MODEL, D_EXPERT), D_MODEL ** -0.5),
        'w_down': nrm(ks[22], (DEPTH, N_EXPERTS, D_EXPERT, D_MODEL), D_EXPERT ** -0.5),
    }


def reference(x_prompt, x_sample, cache_sb_k, cache_sb_v, state_pool, meta_tokens,
              norm_mix, w_in, q_norm, k_norm, pool_mix, pool_scale, w_pool_out, w_sb_out, w_o,
              norm_ffn, w_router_group, b_router_group, w_router_expert, b_router_expert,
              w_gate, w_up, w_down):
    b = x_prompt.shape[0]
    meta = jnp.broadcast_to(meta_tokens.astype(x_prompt.dtype)[None], (b, N_META, D_MODEL))
    hp = jnp.concatenate([meta, x_prompt], axis=1)
    hs = x_sample
    zero_prefix = jnp.zeros((b, POOL_STATE, POOL_WIDTH), x_prompt.dtype)
    kp_l, vp_l, pp_l, ks_l, vs_l, ps_l = [], [], [], [], [], []
    for l in range(DEPTH):
        lw = (norm_mix[l], w_in[l], q_norm[l], k_norm[l], pool_mix[l], pool_scale[l],
              w_pool_out[l], w_sb_out[l], w_o[l], norm_ffn[l],
              w_router_group[l], b_router_group[l], w_router_expert[l], b_router_expert[l],
              w_gate[l], w_up[l], w_down[l])
        hp, kp, vp, pp = trunk_layer(hp, zero_prefix, 0, None, None, *lw)
        hs, ks, vs, ps = trunk_layer(hs, state_pool[l], POOL_STATE, cache_sb_k[l], cache_sb_v[l], *lw)
        kp_l.append(kp)
        vp_l.append(vp)
        pp_l.append(pp)
        ks_l.append(ks)
        vs_l.append(vs)
        ps_l.append(ps)
    return (hp[:, N_META:], hs, jnp.stack(kp_l), jnp.stack(vp_l), jnp.stack(pp_l),
            jnp.stack(ks_l), jnp.stack(vs_l), jnp.stack(ps_l))
```

```python
import functools

import jax
import jax.numpy as jnp
from jax import lax
from jax.experimental import pallas as pl
from jax.experimental.pallas import tpu as pltpu

F32 = jnp.float32
BF16 = jnp.bfloat16

N_META = 16
POOL_WINDOWS = (2, 4, 8, 16)
POOL_GROUP = 128
POOL_WIDTH = 512
POOL_STATE = 15
SB_HEADS = 8
SB_HEAD_DIM = 64
SB_WIDTH = 512
HEAD_PAIR_LANES = 2 * SB_HEAD_DIM
N_GROUPS = 4
EXPERTS_PER_GROUP = 8
N_EXPERTS = 32
ROUTER_LANES = 128
ROW_LANES = 128
NORM_EPS = 1e-6
CARRY_ROWS = 16
KEY_BLOCK = 256
VMEM_LIMIT_BYTES = 56 * 1024 * 1024


def _params(*sem):
    return pltpu.CompilerParams(dimension_semantics=sem, vmem_limit_bytes=VMEM_LIMIT_BYTES)


def _rms(x, gain):
    ms = jnp.mean(x * x, axis=-1, keepdims=True)
    return x * lax.rsqrt(ms + NORM_EPS) * gain


def _inproj_body(x_ref, nm_ref, w_ref, qg_ref, kg_ref, hs_ref, pm_ref, ps_ref, uprev_ref,
                 qb_ref, kb_ref, vb_ref, kf_ref, vf_ref, po_ref, ut_ref, carry_ref, *, tm):
    @pl.when(pl.program_id(1) == 0)
    def _():
        carry_ref[...] = uprev_ref[...]

    hn = _rms(x_ref[...], nm_ref[...]).astype(BF16)
    z = jnp.dot(hn, w_ref[...], preferred_element_type=F32)
    u = z[:, 0:POOL_WIDTH]
    v = z[:, 3 * SB_WIDTH:4 * SB_WIDTH]

    def head_norm(a, gain):
        ms = jnp.dot((a * a).astype(BF16), hs_ref[...], preferred_element_type=F32)
        return a * lax.rsqrt(ms + NORM_EPS) * gain

    qn = head_norm(z[:, SB_WIDTH:2 * SB_WIDTH], qg_ref[...])
    kn = head_norm(z[:, 2 * SB_WIDTH:3 * SB_WIDTH], kg_ref[...])
    qb_ref[...] = (qn * (SB_HEAD_DIM ** -0.5)).astype(BF16)
    kb_ref[...] = kn.astype(BF16)
    vb_ref[...] = v.astype(BF16)
    kf_ref[...] = kn
    vf_ref[...] = v

    ext = jnp.concatenate([carry_ref[...], u], axis=0)
    mixed = []
    for gi, w in enumerate(POOL_WINDOWS):
        c0, c1 = gi * POOL_GROUP, (gi + 1) * POOL_GROUP
        s = ext[:, c0:c1]
        shift = 1
        while shift < w:
            s = s + pltpu.roll(s, shift, axis=0)
            shift *= 2
        pooled = s[CARRY_ROWS:, :] * (1.0 / w) - u[:, c0:c1]
        mixed.append(jnp.dot(pooled.astype(BF16), pm_ref[gi], preferred_element_type=F32))
    po_ref[...] = (jnp.concatenate(mixed, axis=1) * ps_ref[...]).astype(BF16)
    carry_ref[...] = u[tm - CARRY_ROWS:, :]
    ut_ref[...] = u[tm - CARRY_ROWS:, :]


def _inproj(x, uprev, norm_mix, w_uqkv, qg, kg, hsum, pool_mix, pool_scale):
    B, S, D = x.shape
    tm = min(512, S)
    assert S % tm == 0 and tm % CARRY_ROWS == 0
    per_batch_prev = uprev.shape[0] == B and B > 1
    row = lambda b, t: (b, t, 0)
    const2 = lambda b, t: (0, 0)
    out_shape = (
        jax.ShapeDtypeStruct((B, S, SB_WIDTH), BF16),
        jax.ShapeDtypeStruct((B, S, SB_WIDTH), BF16),
        jax.ShapeDtypeStruct((B, S, SB_WIDTH), BF16),
        jax.ShapeDtypeStruct((B, S, SB_WIDTH), F32),
        jax.ShapeDtypeStruct((B, S, SB_WIDTH), F32),
        jax.ShapeDtypeStruct((B, S, POOL_WIDTH), BF16),
        jax.ShapeDtypeStruct((B, CARRY_ROWS, POOL_WIDTH), F32),
    )
    blk = lambda w: pl.BlockSpec((None, tm, w), row)
    return pl.pallas_call(
        functools.partial(_inproj_body, tm=tm),
        grid=(B, S // tm),
        in_specs=[
            pl.BlockSpec((None, tm, D), row),
            pl.BlockSpec((1, D), const2),
            pl.BlockSpec((D, 4 * SB_WIDTH), const2),
            pl.BlockSpec((1, SB_WIDTH), const2),
            pl.BlockSpec((1, SB_WIDTH), const2),
            pl.BlockSpec((SB_WIDTH, SB_WIDTH), const2),
            pl.BlockSpec((len(POOL_WINDOWS), POOL_GROUP, POOL_GROUP), lambda b, t: (0, 0, 0)),
            pl.BlockSpec((1, POOL_WIDTH), const2),
            pl.BlockSpec((None, CARRY_ROWS, POOL_WIDTH),
                         (lambda b, t: (b, 0, 0)) if per_batch_prev else (lambda b, t: (0, 0, 0))),
        ],
        out_specs=[blk(SB_WIDTH), blk(SB_WIDTH), blk(SB_WIDTH), blk(SB_WIDTH), blk(SB_WIDTH), blk(POOL_WIDTH),
                   pl.BlockSpec((None, CARRY_ROWS, POOL_WIDTH), lambda b, t: (b, 0, 0))],
        out_shape=out_shape,
        scratch_shapes=[pltpu.VMEM((CARRY_ROWS, POOL_WIDTH), F32)],
        compiler_params=_params("arbitrary", "arbitrary"),
        name="inproj",
    )(x, norm_mix, w_uqkv, qg, kg, hsum, pool_mix, pool_scale, uprev)


def _softplus(s):
    return jnp.maximum(s, 0.0) + jnp.log(1.0 + jnp.exp(-jnp.abs(s)))


def _dot_nt(a, b):
    return lax.dot_general(a, b, (((1,), (1,)), ((), ())), preferred_element_type=F32)


def _attn_body(q_ref, k_ref, v_ref, kp_ref, vp_ref, tri_ref, o_ref, *, L, P, tq, tp):
    lane = lax.broadcasted_iota(jnp.int32, (1, HEAD_PAIR_LANES), 1)
    rows = lax.broadcasted_iota(jnp.int32, (tq, tq), 0)
    cols = lax.broadcasted_iota(jnp.int32, (tq, tq), 1)
    visible = cols < rows

    def visit(q, kblk, vblk, tk, acc, run):
        z = _dot_nt(q, kblk)
        sp = _softplus(z)
        later = jnp.dot(sp.astype(BF16), tri_ref[0:tk, 0:tk], preferred_element_type=F32)
        a = jnp.exp(z - sp + later + run)
        acc = acc + jnp.dot(a.astype(BF16), vblk, preferred_element_type=F32)
        return acc, run - jnp.sum(sp, axis=1, keepdims=True)

    def one_head(qs, qi, h):
        in_head = (lane >= h * SB_HEAD_DIM) & (lane < (h + 1) * SB_HEAD_DIM)
        q = jnp.where(in_head, q_ref[pl.ds(qs, tq), :], jnp.zeros((), BF16))
        z = _dot_nt(q, k_ref[pl.ds(qs, tq), :])
        sp = jnp.where(visible, _softplus(z), 0.0)
        later = jnp.dot(sp.astype(BF16), tri_ref[0:tq, 0:tq], preferred_element_type=F32)
        a = jnp.where(visible, jnp.exp(z - sp + later), 0.0)
        acc = jnp.dot(a.astype(BF16), v_ref[pl.ds(qs, tq), :], preferred_element_type=F32)
        run = -jnp.sum(sp, axis=1, keepdims=True)

        def own_block(i, carry):
            ks = pl.multiple_of((qi - 1 - i) * tq, tq)
            return visit(q, k_ref[pl.ds(ks, tq), :], v_ref[pl.ds(ks, tq), :], tq, *carry)

        acc, run = lax.fori_loop(0, qi, own_block, (acc, run))

        def prefix_block(i, carry):
            ks = pl.multiple_of((P // tp - 1 - i) * tp, tp)
            return visit(q, kp_ref[pl.ds(ks, tp), :], vp_ref[pl.ds(ks, tp), :], tp, *carry)

        acc, run = lax.fori_loop(0, P // tp, prefix_block, (acc, run))
        return acc

    def q_block(qi, carry):
        qs = pl.multiple_of(qi * tq, tq)
        acc0 = one_head(qs, qi, 0)
        acc1 = one_head(qs, qi, 1)
        o_ref[pl.ds(qs, tq), :] = jnp.where(lane < SB_HEAD_DIM, acc0, acc1).astype(BF16)
        return carry

    lax.fori_loop(0, L // tq, q_block, 0)


def _attention(q, k, v, kp, vp, tri):
    B, L, _ = q.shape
    P = kp.shape[1]
    tq = min(KEY_BLOCK, L)
    tp = min(KEY_BLOCK, P)
    assert L % tq == 0 and P % tp == 0 and tq % 16 == 0 and tp % 16 == 0
    per_batch_prefix = kp.shape[0] == B and B > 1
    seq = pl.BlockSpec((None, L, HEAD_PAIR_LANES), lambda b, hp: (b, 0, hp))
    pre = pl.BlockSpec((None, P, HEAD_PAIR_LANES),
                       (lambda b, hp: (b, 0, hp)) if per_batch_prefix else (lambda b, hp: (0, 0, hp)))
    return pl.pallas_call(
        functools.partial(_attn_body, L=L, P=P, tq=tq, tp=tp),
        grid=(B, SB_WIDTH // HEAD_PAIR_LANES),
        in_specs=[seq, seq, seq, pre, pre, pl.BlockSpec((KEY_BLOCK, KEY_BLOCK), lambda b, hp: (0, 0))],
        out_specs=seq,
        out_shape=jax.ShapeDtypeStruct((B, L, SB_WIDTH), BF16),
        compiler_params=_params("arbitrary", "arbitrary"),
        name="stickbreak",
    )(q, k, v, kp, vp, tri)


def _store_rows(ref, val):
    for c in range(val.shape[1] // ROW_LANES):
        ref[:, c, :] = val[:, c * ROW_LANES:(c + 1) * ROW_LANES]


def _load_rows(ref):
    return jnp.concatenate([ref[:, c, :] for c in range(ref.shape[1])], axis=1)


def _merge_body(x_ref, po_ref, sb_ref, nm_ref, wg_ref, wpo_ref, wsb_ref, wo_ref, nf_ref, wr_ref, br_ref, low_ref,
                h2_ref, hn2_ref, ri_ref, rw_ref, cnt_ref, carry_ref, *, tm):
    @pl.when(pl.program_id(0) == 0)
    def _():
        carry_ref[...] = jnp.zeros_like(carry_ref)

    x = x_ref[...]
    D = x.shape[1]
    hn = _rms(x, nm_ref[...]).astype(BF16)
    gates = jnp.dot(hn, wg_ref[...], preferred_element_type=F32)
    pool_proj = jnp.dot(po_ref[...], wpo_ref[...], preferred_element_type=F32)
    sb_proj = jnp.dot(sb_ref[...], wsb_ref[...], preferred_element_type=F32)
    merged = jax.nn.sigmoid(gates[:, :D]) * pool_proj + jax.nn.sigmoid(gates[:, D:]) * sb_proj
    h2 = x + jnp.dot(merged.astype(BF16), wo_ref[...], preferred_element_type=F32)
    h2_ref[...] = h2
    hn2 = _rms(h2, nf_ref[...])
    _store_rows(hn2_ref, hn2)

    logits = jnp.dot(hn2.astype(BF16), wr_ref[...], preferred_element_type=F32) + br_ref[...]
    lane = lax.broadcasted_iota(jnp.int32, logits.shape, 1)
    lanef = lane.astype(F32)
    neg = jnp.float32(-jnp.inf)
    big = jnp.float32(2 * ROUTER_LANES)

    def first_argmax(vals):
        m = jnp.max(vals, axis=1, keepdims=True)
        return m, jnp.min(jnp.where(vals == m, lanef, big), axis=1, keepdims=True)

    is_group = lane < N_GROUPS
    gmax, gidx = first_argmax(jnp.where(is_group, logits, neg))
    p_group = 1.0 / jnp.sum(jnp.where(is_group, jnp.exp(logits - gmax), 0.0), axis=1, keepdims=True)
    first = N_GROUPS + EXPERTS_PER_GROUP * gidx
    in_group = (lanef >= first) & (lanef < first + EXPERTS_PER_GROUP)
    cand = jnp.where(in_group, logits, neg)
    m1, i1 = first_argmax(cand)
    m2, i2 = first_argmax(jnp.where(lanef == i1, neg, cand))
    t = jnp.exp(m2 - m1)
    w1 = p_group / (1.0 + t)
    w2 = p_group * t / (1.0 + t)

    hit1 = lanef == i1
    hit2 = lanef == i2
    onehot = jnp.where(hit1 | hit2, 1.0, 0.0)
    before = jnp.dot(low_ref[...], onehot.astype(BF16), preferred_element_type=F32) + carry_ref[...]
    r1 = jnp.sum(jnp.where(hit1, before, 0.0), axis=1, keepdims=True)
    r2 = jnp.sum(jnp.where(hit2, before, 0.0), axis=1, keepdims=True)
    carry_ref[...] = carry_ref[...] + jnp.sum(onehot, axis=0, keepdims=True)
    cnt_ref[...] = carry_ref[...]

    slab = jnp.where(lane == 0, i1 - N_GROUPS,
                     jnp.where(lane == 1, i2 - N_GROUPS, jnp.where(lane == 2, r1, jnp.where(lane == 3, r2, 0.0))))
    ri_ref[...] = slab.astype(jnp.int32)
    rw_ref[...] = jnp.where(lane == 0, w1, jnp.where(lane == 1, w2, 0.0))


def _merge_route(x, po, sb, norm_mix, w_gates, w_pool_out, w_sb_out, w_o, norm_ffn, w_router, b_router, lower):
    N, D = x.shape
    tm = lower.shape[0]
    assert N % tm == 0
    row = lambda i: (i, 0)
    const = lambda i: (0, 0)
    full = lambda a: pl.BlockSpec(a.shape, const)
    return pl.pallas_call(
        functools.partial(_merge_body, tm=tm),
        grid=(N // tm,),
        in_specs=[pl.BlockSpec((tm, D), row), pl.BlockSpec((tm, POOL_WIDTH), row), pl.BlockSpec((tm, SB_WIDTH), row),
                  full(norm_mix), full(w_gates), full(w_pool_out), full(w_sb_out), full(w_o), full(norm_ffn),
                  full(w_router), full(b_router), full(lower)],
        out_specs=[pl.BlockSpec((tm, D), row), pl.BlockSpec((tm, D // ROW_LANES, ROW_LANES), lambda i: (i, 0, 0)),
                   pl.BlockSpec((tm, ROUTER_LANES), row), pl.BlockSpec((tm, ROUTER_LANES), row),
                   pl.BlockSpec((1, ROUTER_LANES), const)],
        out_shape=(jax.ShapeDtypeStruct((N, D), F32), jax.ShapeDtypeStruct((N, D // ROW_LANES, ROW_LANES), F32),
                   jax.ShapeDtypeStruct((N, ROUTER_LANES), jnp.int32), jax.ShapeDtypeStruct((N, ROUTER_LANES), F32),
                   jax.ShapeDtypeStruct((1, ROUTER_LANES), F32)),
        scratch_shapes=[pltpu.VMEM((1, ROUTER_LANES), F32)],
        compiler_params=_params("arbitrary"),
        name="merge_route",
    )(x, po, sb, norm_mix, w_gates, w_pool_out, w_sb_out, w_o, norm_ffn, w_router, b_router, lower)


def _row_copy(src_ref, src_row, dst_ref, dst_row, sem):
    return pltpu.make_async_copy(src_ref.at[src_row], dst_ref.at[dst_row], sem)


def _dispatch_body(pend_ref, padded_ref, dest_ref, x_ref, buf_ref, zero_ref, sem, zsem, *, tm, bm):
    @pl.when(pl.program_id(0) == 0)
    def _():
        zero_ref[...] = jnp.zeros_like(zero_ref)
        for e in range(N_EXPERTS):
            @pl.when(padded_ref[e] > 0)
            def _():
                cp = pltpu.make_async_copy(zero_ref, buf_ref.at[pl.ds(pend_ref[e] - bm, bm)], zsem)
                cp.start()
                cp.wait()

        def zero_unused_block(bi, c):
            cp = pltpu.make_async_copy(zero_ref, buf_ref.at[pl.ds(bi * bm, bm)], zsem)
            cp.start()
            cp.wait()
            return c

        lax.fori_loop(lax.div(pend_ref[N_EXPERTS - 1], bm), buf_ref.shape[0] // bm, zero_unused_block, 0)

    def issue(r, c):
        _row_copy(x_ref, r, buf_ref, dest_ref[0, 2 * r], sem).start()
        _row_copy(x_ref, r, buf_ref, dest_ref[0, 2 * r + 1], sem).start()
        return c

    lax.fori_loop(0, tm, issue, 0)

    def drain(r, c):
        _row_copy(x_ref, 0, buf_ref, 0, sem).wait()
        _row_copy(x_ref, 0, buf_ref, 0, sem).wait()
        return c

    lax.fori_loop(0, tm, drain, 0)


def _dispatch(hn2, dest, pend, padded, cap, bm, tm):
    N, C, W = hn2.shape
    grid_spec = pltpu.PrefetchScalarGridSpec(
        num_scalar_prefetch=2,
        grid=(N // tm,),
        in_specs=[pl.BlockSpec((None, 1, 2 * tm), lambda i, pe, pa: (i, 0, 0), memory_space=pltpu.SMEM),
                  pl.BlockSpec((tm, C, W), lambda i, pe, pa: (i, 0, 0))],
        out_specs=pl.BlockSpec(memory_space=pl.ANY),
        scratch_shapes=[pltpu.VMEM((bm, C, W), F32), pltpu.SemaphoreType.DMA, pltpu.SemaphoreType.DMA],
    )
    return pl.pallas_call(
        functools.partial(_dispatch_body, tm=tm, bm=bm),
        grid_spec=grid_spec,
        out_shape=jax.ShapeDtypeStruct((cap, C, W), F32),
        compiler_params=_params("arbitrary"),
        name="dispatch",
    )(pend, padded, dest.reshape(N // tm, 1, 2 * tm), hn2)


def _expert_body(blk_e_ref, nact_ref, x_ref, wg_ref, wu_ref, wd_ref, y_ref):
    @pl.when(pl.program_id(0) < nact_ref[0])
    def _():
        x = _load_rows(x_ref).astype(BF16)
        gate = jnp.dot(x, wg_ref[...], preferred_element_type=F32)
        up = jnp.dot(x, wu_ref[...], preferred_element_type=F32)
        hidden = (jax.nn.silu(gate) * up).astype(BF16)
        _store_rows(y_ref, jnp.dot(hidden, wd_ref[...], preferred_element_type=F32))

    @pl.when(pl.program_id(0) >= nact_ref[0])
    def _():
        y_ref[...] = jnp.zeros_like(y_ref)


def _experts(buf, blk_e, nact, w_gate, w_up, w_down, bm):
    cap, C, W = buf.shape
    D, DE = w_gate.shape[1], w_gate.shape[2]
    rows = lambda i, be, na: (jnp.minimum(i, na[0] - 1), 0, 0)
    grid_spec = pltpu.PrefetchScalarGridSpec(
        num_scalar_prefetch=2,
        grid=(cap // bm,),
        in_specs=[pl.BlockSpec((bm, C, W), rows),
                  pl.BlockSpec((None, D, DE), lambda i, be, na: (be[i], 0, 0)),
                  pl.BlockSpec((None, D, DE), lambda i, be, na: (be[i], 0, 0)),
                  pl.BlockSpec((None, DE, D), lambda i, be, na: (be[i], 0, 0))],
        out_specs=pl.BlockSpec((bm, C, W), lambda i, be, na: (i, 0, 0)),
    )
    return pl.pallas_call(
        _expert_body,
        grid_spec=grid_spec,
        out_shape=jax.ShapeDtypeStruct((cap, C, W), F32),
        compiler_params=_params("arbitrary"),
        name="experts",
    )(blk_e, nact, buf, w_gate, w_up, w_down)


def _combine_body(dest_ref, h2_ref, rw_ref, yb_ref, y_ref, rows_ref, sem, *, tm):
    def issue(r, c):
        _row_copy(yb_ref, dest_ref[0, 2 * r], rows_ref.at[0], r, sem).start()
        _row_copy(yb_ref, dest_ref[0, 2 * r + 1], rows_ref.at[1], r, sem).start()
        return c

    lax.fori_loop(0, tm, issue, 0)

    def drain(r, c):
        _row_copy(yb_ref, 0, rows_ref.at[0], 0, sem).wait()
        _row_copy(yb_ref, 0, rows_ref.at[1], 0, sem).wait()
        return c

    lax.fori_loop(0, tm, drain, 0)
    w = rw_ref[...]
    y_ref[...] = h2_ref[...] + (_load_rows(rows_ref.at[0]) * w[:, 0:1] + _load_rows(rows_ref.at[1]) * w[:, 1:2])


def _combine(h2, rw, dest, yb, tm):
    N, D = h2.shape
    C, W = yb.shape[1:]
    return pl.pallas_call(
        functools.partial(_combine_body, tm=tm),
        grid=(N // tm,),
        in_specs=[pl.BlockSpec((None, 1, 2 * tm), lambda i: (i, 0, 0), memory_space=pltpu.SMEM),
                  pl.BlockSpec((tm, D), lambda i: (i, 0)),
                  pl.BlockSpec((tm, ROUTER_LANES), lambda i: (i, 0)),
                  pl.BlockSpec(memory_space=pl.ANY)],
        out_specs=pl.BlockSpec((tm, D), lambda i: (i, 0)),
        out_shape=jax.ShapeDtypeStruct((N, D), F32),
        scratch_shapes=[pltpu.VMEM((2, tm, C, W), F32), pltpu.SemaphoreType.DMA],
        compiler_params=_params("arbitrary"),
        name="combine",
    )(dest.reshape(N // tm, 1, 2 * tm), h2, rw, yb)


def _moe(h2, hn2, ri, rw, counts, w_gate, w_up, w_down, tm, bm):
    N, D = h2.shape
    cnt = counts[0, N_GROUPS:N_GROUPS + N_EXPERTS].astype(jnp.int32)
    padded = (cnt + bm - 1) // bm * bm
    pend = jnp.cumsum(padded)
    pstart = pend - padded
    experts, rank = ri[:, 0:2], ri[:, 2:4]
    dest = (pstart[experts] + rank).astype(jnp.int32)
    cap = -(-(2 * N + N_EXPERTS * (bm - 1)) // bm) * bm
    nblk = cap // bm
    blk_e = jnp.minimum(jnp.searchsorted(pend, jnp.arange(nblk, dtype=jnp.int32) * bm, side='right'),
                        N_EXPERTS - 1).astype(jnp.int32)
    nact = (pend[-1:] // bm).astype(jnp.int32)
    buf = _dispatch(hn2, dest, pend.astype(jnp.int32), padded.astype(jnp.int32), cap, bm, tm)
    yb = _experts(buf, blk_e, nact, w_gate, w_up, w_down, bm)
    return _combine(h2, rw, dest, yb, tm)


def _constants(tm):
    r = lax.broadcasted_iota(jnp.int32, (KEY_BLOCK, KEY_BLOCK), 0)
    c = lax.broadcasted_iota(jnp.int32, (KEY_BLOCK, KEY_BLOCK), 1)
    tri = jnp.where(r > c, -1.0, 0.0).astype(BF16)
    hr = lax.broadcasted_iota(jnp.int32, (SB_WIDTH, SB_WIDTH), 0) // SB_HEAD_DIM
    hc = lax.broadcasted_iota(jnp.int32, (SB_WIDTH, SB_WIDTH), 1) // SB_HEAD_DIM
    hsum = jnp.where(hr == hc, 1.0 / SB_HEAD_DIM, 0.0).astype(BF16)
    lr = lax.broadcasted_iota(jnp.int32, (tm, tm), 0)
    lc = lax.broadcasted_iota(jnp.int32, (tm, tm), 1)
    lower = jnp.where(lc < lr, 1.0, 0.0).astype(BF16)
    return tri, hsum, lower


def kernel(x_prompt, x_sample, cache_sb_k, cache_sb_v, state_pool, meta_tokens, norm_mix, w_in, q_norm, k_norm,
           pool_mix, pool_scale, w_pool_out, w_sb_out, w_o, norm_ffn, w_router_group, b_router_group,
           w_router_expert, b_router_expert, w_gate, w_up, w_down):
    assert norm_mix.shape[0] == 1, "single-layer trunk"
    B, S, D = x_prompt.shape
    BS, SS, _ = x_sample.shape
    past = cache_sb_k.shape[2]
    tm_p, tm_s = min(512, B * S), min(512, BS * SS)
    tri, hsum, lower_p = _constants(tm_p)
    lower_s = lower_p if tm_s == tm_p else _constants(tm_s)[2]

    w_in0 = w_in[0]
    w_uqkv = w_in0[:, :4 * SB_WIDTH].astype(BF16)
    w_gates = w_in0[:, 4 * SB_WIDTH:].astype(BF16)
    nm = norm_mix[0][None, :]
    nf = norm_ffn[0][None, :]
    qg = jnp.tile(q_norm[0], SB_HEADS)[None, :]
    kg = jnp.tile(k_norm[0], SB_HEADS)[None, :]
    pm = pool_mix[0].astype(BF16)
    ps = pool_scale[0][None, :]
    wpo = w_pool_out[0].astype(BF16)
    wsb = w_sb_out[0].astype(BF16)
    wo = w_o[0].astype(BF16)
    pad = ROUTER_LANES - N_GROUPS - N_EXPERTS
    w_router = jnp.concatenate([w_router_group[0], w_router_expert[0], jnp.zeros((D, pad), F32)], axis=1).astype(BF16)
    b_router = jnp.concatenate([b_router_group[0], b_router_expert[0], jnp.zeros((pad,), F32)])[None, :]
    wg_e, wu_e, wd_e = w_gate[0].astype(BF16), w_up[0].astype(BF16), w_down[0].astype(BF16)

    inproj = functools.partial(_inproj, norm_mix=nm, w_uqkv=w_uqkv, qg=qg, kg=kg, hsum=hsum, pool_mix=pm,
                               pool_scale=ps)

    def tail(x, po, sb, lower, bm):
        n = x.shape[0] * x.shape[1]
        h2, hn2, ri, rw, counts = _merge_route(x.reshape(n, D), po.reshape(n, POOL_WIDTH), sb.reshape(n, SB_WIDTH),
                                               nm, w_gates, wpo, wsb, wo, nf, w_router, b_router, lower)
        return _moe(h2, hn2, ri, rw, counts, wg_e, wu_e, wd_e, lower.shape[0], bm).reshape(x.shape)

    shape5 = lambda a: a.reshape(1, a.shape[0], a.shape[1], SB_HEADS, SB_HEAD_DIM)

    prev_s = jnp.concatenate([jnp.zeros((BS, CARRY_ROWS - POOL_STATE, POOL_WIDTH), F32), state_pool[0]], axis=1)
    q_s, k_s, v_s, ks_f, vs_f, po_s, utail_s = inproj(x_sample, prev_s)
    kc = cache_sb_k[0].reshape(BS, past, SB_WIDTH).astype(BF16)
    vc = cache_sb_v[0].reshape(BS, past, SB_WIDTH).astype(BF16)
    sb_s = _attention(q_s, k_s, v_s, kc, vc, tri)
    y_sample = tail(x_sample, po_s, sb_s, lower_s, 128)
    new_pool_sample = utail_s[None, :, CARRY_ROWS - POOL_STATE:, :]

    zeros_prev = jnp.zeros((1, CARRY_ROWS, POOL_WIDTH), F32)
    _, km_b, vm_b, km_f, vm_f, _, u_meta = inproj(meta_tokens[None], zeros_prev)

    q_b, k_b, v_b, k_f, v_f, po, utail = inproj(x_prompt, u_meta)
    sb = _attention(q_b, k_b, v_b, km_b, vm_b, tri)
    y_prompt = tail(x_prompt, po, sb, lower_p, 512)
    new_k_prompt = shape5(jnp.concatenate([jnp.broadcast_to(km_f, (B, N_META, SB_WIDTH)), k_f], axis=1))
    new_v_prompt = shape5(jnp.concatenate([jnp.broadcast_to(vm_f, (B, N_META, SB_WIDTH)), v_f], axis=1))
    new_pool_prompt = utail[None, :, CARRY_ROWS - POOL_STATE:, :]

    return (y_prompt, y_sample, new_k_prompt, new_v_prompt, new_pool_prompt,
            shape5(ks_f), shape5(vs_f), new_pool_sample)
```

```python
import functools

import jax
import jax.numpy as jnp
from jax import lax
from jax.experimental import pallas as pl
from jax.experimental.pallas import tpu as pltpu

F32 = jnp.float32
BF16 = jnp.bfloat16

N_META = 16
POOL_WINDOWS = (2, 4, 8, 16)
POOL_GROUP = 128
POOL_WIDTH = 512
POOL_STATE = 15
SB_HEADS = 8
SB_HEAD_DIM = 64
SB_WIDTH = 512
HEAD_PAIR_LANES = 2 * SB_HEAD_DIM
N_GROUPS = 4
EXPERTS_PER_GROUP = 8
N_EXPERTS = 32
ROUTER_LANES = 128
ROW_LANES = 128
NORM_EPS = 1e-6
CARRY_ROWS = 16
KEY_BLOCK = 256
VMEM_LIMIT_BYTES = 56 * 1024 * 1024


def _params(*sem):
    return pltpu.CompilerParams(dimension_semantics=sem, vmem_limit_bytes=VMEM_LIMIT_BYTES)


def _rms(x, gain):
    ms = jnp.mean(x * x, axis=-1, keepdims=True)
    return x * lax.rsqrt(ms + NORM_EPS) * gain


def _inproj_body(x_ref, nm_ref, w_ref, qg_ref, kg_ref, hs_ref, pm_ref, ps_ref, uprev_ref,
                 qb_ref, kb_ref, vb_ref, kf_ref, vf_ref, po_ref, ut_ref, carry_ref, *, tm):
    @pl.when(pl.program_id(1) == 0)
    def _():
        carry_ref[...] = uprev_ref[...]

    hn = _rms(x_ref[...], nm_ref[...]).astype(BF16)
    z = jnp.dot(hn, w_ref[...], preferred_element_type=F32)
    u = z[:, 0:POOL_WIDTH]
    v = z[:, 3 * SB_WIDTH:4 * SB_WIDTH]

    def head_norm(a, gain):
        ms = jnp.dot((a * a).astype(BF16), hs_ref[...], preferred_element_type=F32)
        return a * lax.rsqrt(ms + NORM_EPS) * gain

    qn = head_norm(z[:, SB_WIDTH:2 * SB_WIDTH], qg_ref[...])
    kn = head_norm(z[:, 2 * SB_WIDTH:3 * SB_WIDTH], kg_ref[...])
    qb_ref[...] = (qn * (SB_HEAD_DIM ** -0.5)).astype(BF16)
    kb_ref[...] = kn.astype(BF16)
    vb_ref[...] = v.astype(BF16)
    kf_ref[...] = kn
    vf_ref[...] = v

    ext = jnp.concatenate([carry_ref[...], u], axis=0)
    mixed = []
    for gi, w in enumerate(POOL_WINDOWS):
        c0, c1 = gi * POOL_GROUP, (gi + 1) * POOL_GROUP
        s = ext[:, c0:c1]
        shift = 1
        while shift < w:
            s = s + pltpu.roll(s, shift, axis=0)
            shift *= 2
        pooled = s[CARRY_ROWS:, :] * (1.0 / w) - u[:, c0:c1]
        mixed.append(jnp.dot(pooled.astype(BF16), pm_ref[gi], preferred_element_type=F32))
    po_ref[...] = (jnp.concatenate(mixed, axis=1) * ps_ref[...]).astype(BF16)
    carry_ref[...] = u[tm - CARRY_ROWS:, :]
    ut_ref[...] = u[tm - CARRY_ROWS:, :]


def _inproj(x, uprev, norm_mix, w_uqkv, qg, kg, hsum, pool_mix, pool_scale):
    B, S, D = x.shape
    tm = min(512, S)
    assert S % tm == 0 and tm % CARRY_ROWS == 0
    per_batch_prev = uprev.shape[0] == B and B > 1
    row = lambda b, t: (b, t, 0)
    const2 = lambda b, t: (0, 0)
    out_shape = (
        jax.ShapeDtypeStruct((B, S, SB_WIDTH), BF16),
        jax.ShapeDtypeStruct((B, S, SB_WIDTH), BF16),
        jax.ShapeDtypeStruct((B, S, SB_WIDTH), BF16),
        jax.ShapeDtypeStruct((B, S, SB_WIDTH), F32),
        jax.ShapeDtypeStruct((B, S, SB_WIDTH), F32),
        jax.ShapeDtypeStruct((B, S, POOL_WIDTH), BF16),
        jax.ShapeDtypeStruct((B, CARRY_ROWS, POOL_WIDTH), F32),
    )
    blk = lambda w: pl.BlockSpec((None, tm, w), row)
    return pl.pallas_call(
        functools.partial(_inproj_body, tm=tm),
        grid=(B, S // tm),
        in_specs=[
            pl.BlockSpec((None, tm, D), row),
            pl.BlockSpec((1, D), const2),
            pl.BlockSpec((D, 4 * SB_WIDTH), const2),
            pl.BlockSpec((1, SB_WIDTH), const2),
            pl.BlockSpec((1, SB_WIDTH), const2),
            pl.BlockSpec((SB_WIDTH, SB_WIDTH), const2),
            pl.BlockSpec((len(POOL_WINDOWS), POOL_GROUP, POOL_GROUP), lambda b, t: (0, 0, 0)),
            pl.BlockSpec((1, POOL_WIDTH), const2),
            pl.BlockSpec((None, CARRY_ROWS, POOL_WIDTH),
                         (lambda b, t: (b, 0, 0)) if per_batch_prev else (lambda b, t: (0, 0, 0))),
        ],
        out_specs=[blk(SB_WIDTH), blk(SB_WIDTH), blk(SB_WIDTH), blk(SB_WIDTH), blk(SB_WIDTH), blk(POOL_WIDTH),
                   pl.BlockSpec((None, CARRY_ROWS, POOL_WIDTH), lambda b, t: (b, 0, 0))],
        out_shape=out_shape,
        scratch_shapes=[pltpu.VMEM((CARRY_ROWS, POOL_WIDTH), F32)],
        compiler_params=_params("arbitrary", "arbitrary"),
        name="inproj",
    )(x, norm_mix, w_uqkv, qg, kg, hsum, pool_mix, pool_scale, uprev)


def _log_gates(z):
    sign = jnp.uint32(0x80000000)
    neg_abs = lax.bitcast_convert_type(lax.bitcast_convert_type(z, jnp.uint32) | sign, F32)
    sp = jnp.maximum(z, 0.0) + jnp.log(1.0 + jnp.exp(neg_abs))
    return sp, z - sp


def _dot_nt(a, b):
    return lax.dot_general(a, b, (((1,), (1,)), ((), ())), preferred_element_type=F32)


def _attn_body(vq_ref, vk_ref, q_ref, k_ref, v_ref, kp_ref, vp_ref, tri_ref, o_ref,
               z_scr, e_scr, acc_scr, run_scr, *, L, P, tq, n_visits, full_from_prefix):
    nq = L // tq
    lane = lax.broadcasted_iota(jnp.int32, (1, HEAD_PAIR_LANES), 1)
    heads = (lane < SB_HEAD_DIM, lane >= SB_HEAD_DIM)
    ksrc, vsrc = (kp_ref, vp_ref) if full_from_prefix else (k_ref, v_ref)

    def stacked_q(qi):
        qp = q_ref[pl.ds(pl.multiple_of(qi * tq, tq), tq), :]
        zero = jnp.zeros((), BF16)
        return jnp.concatenate([jnp.where(heads[0], qp, zero), jnp.where(heads[1], qp, zero)], axis=0)

    def key_rows(s):
        return pl.ds(pl.multiple_of(vk_ref[s] * KEY_BLOCK, KEY_BLOCK), KEY_BLOCK)

    def stage_a(s):
        z_scr[...] = _dot_nt(stacked_q(vq_ref[s]), ksrc[key_rows(s), :].astype(BF16))

    def stage_b(s):
        qi = vq_ref[s]
        sp, ls = _log_gates(z_scr[...])
        later = jnp.dot(sp.astype(BF16), tri_ref[...], preferred_element_type=F32)
        run = run_scr[qi]
        e_scr[...] = ls + (later + jnp.concatenate([run, run], axis=1))
        run_scr[qi] = run - jnp.sum(sp, axis=1, keepdims=True)

    def stage_c(s):
        qi = vq_ref[s]
        a = jnp.exp(e_scr[...]).astype(BF16)
        acc_scr[qi] = acc_scr[qi] + jnp.dot(a, vsrc[key_rows(s), :].astype(BF16), preferred_element_type=F32)

    if n_visits:
        acc_scr[...] = jnp.zeros_like(acc_scr)
        run_scr[...] = jnp.zeros_like(run_scr)
        stage_a(0)
        if n_visits >= 2:
            stage_b(0)
            stage_a(1)

        def steady(s, c):
            stage_c(s - 2)
            stage_b(s - 1)
            stage_a(s)
            return c

        lax.fori_loop(2, n_visits, steady, 0)
        if n_visits >= 2:
            stage_c(n_visits - 2)
        stage_b(n_visits - 1)
        stage_c(n_visits - 1)

    rows = lax.broadcasted_iota(jnp.int32, (2 * tq, tq), 0)
    cols = lax.broadcasted_iota(jnp.int32, (2 * tq, tq), 1)
    visible = (cols < rows) & (rows < tq) | (cols < rows - tq)

    def q_block(qi, carry):
        qrows = pl.ds(pl.multiple_of(qi * tq, tq), tq)
        q2 = stacked_q(qi)
        sp, ls = _log_gates(_dot_nt(q2, k_ref[qrows, :]))
        sp = jnp.where(visible, sp, 0.0)
        later = jnp.dot(sp.astype(BF16), tri_ref[0:tq, 0:tq], preferred_element_type=F32)
        a = jnp.where(visible, jnp.exp(ls + later), 0.0)
        acc = jnp.dot(a.astype(BF16), v_ref[qrows, :], preferred_element_type=F32)
        run = -jnp.sum(sp, axis=1, keepdims=True)
        if n_visits:
            acc = acc + jnp.exp(run) * acc_scr[qi]
            run = run + run_scr[qi][:, 0:1]
        if not full_from_prefix:
            sp, ls = _log_gates(_dot_nt(q2, kp_ref[...].astype(BF16)))
            later = jnp.dot(sp.astype(BF16), tri_ref[0:P, 0:P], preferred_element_type=F32)
            a = jnp.exp(ls + (later + run))
            acc = acc + jnp.dot(a.astype(BF16), vp_ref[...].astype(BF16), preferred_element_type=F32)
        o_ref[qrows, :] = jnp.where(heads[0], acc[:tq], acc[tq:]).astype(BF16)
        return carry

    lax.fori_loop(0, nq, q_block, 0)


def _attention(q, k, v, kp, vp, tri):
    B, L, _ = q.shape
    P = kp.shape[1]
    tq = min(KEY_BLOCK, L)
    nq = L // tq
    assert L % tq == 0 and tq % 16 == 0 and P % 16 == 0
    full_from_prefix = P > KEY_BLOCK
    if full_from_prefix:
        assert nq == 1 and P % KEY_BLOCK == 0
        visits = [(qi, kb) for qi in range(nq) for kb in reversed(range(P // KEY_BLOCK))]
    else:
        assert tq == KEY_BLOCK or nq == 1
        visits = [(qi, kb) for qi in range(nq) for kb in reversed(range(qi))]
    n_visits = len(visits)
    vq = jnp.asarray([qi for qi, _ in visits] or [0], jnp.int32)
    vk = jnp.asarray([kb for _, kb in visits] or [0], jnp.int32)
    per_batch_prefix = kp.shape[0] == B and B > 1
    seq = pl.BlockSpec((None, L, HEAD_PAIR_LANES), lambda b, hp, *_: (b, 0, hp))
    pre = pl.BlockSpec((None, P, HEAD_PAIR_LANES),
                       (lambda b, hp, *_: (b, 0, hp)) if per_batch_prefix else (lambda b, hp, *_: (0, 0, hp)))
    grid_spec = pltpu.PrefetchScalarGridSpec(
        num_scalar_prefetch=2,
        grid=(B, SB_WIDTH // HEAD_PAIR_LANES),
        in_specs=[seq, seq, seq, pre, pre, pl.BlockSpec((KEY_BLOCK, KEY_BLOCK), lambda b, hp, *_: (0, 0))],
        out_specs=seq,
        scratch_shapes=[pltpu.VMEM((2 * tq, KEY_BLOCK), F32), pltpu.VMEM((2 * tq, KEY_BLOCK), F32),
                        pltpu.VMEM((nq, 2 * tq, HEAD_PAIR_LANES), F32),
                        pltpu.VMEM((nq, 2 * tq, HEAD_PAIR_LANES), F32)],
    )
    return pl.pallas_call(
        functools.partial(_attn_body, L=L, P=P, tq=tq, n_visits=n_visits, full_from_prefix=full_from_prefix),
        grid_spec=grid_spec,
        out_shape=jax.ShapeDtypeStruct((B, L, SB_WIDTH), BF16),
        compiler_params=_params("arbitrary", "arbitrary"),
        name="stickbreak",
    )(vq, vk, q, k, v, kp, vp, tri)


def _store_rows(ref, val):
    for c in range(val.shape[1] // ROW_LANES):
        ref[:, c, :] = val[:, c * ROW_LANES:(c + 1) * ROW_LANES]


def _load_rows(ref):
    return jnp.concatenate([ref[:, c, :] for c in range(ref.shape[1])], axis=1)


def _merge_body(x_ref, po_ref, sb_ref, nm_ref, wg_ref, wpo_ref, wsb_ref, wo_ref, nf_ref, wr_ref, br_ref, low_ref,
                h2_ref, hn2_ref, ri_ref, rw_ref, cnt_ref, carry_ref, *, tm):
    @pl.when(pl.program_id(0) == 0)
    def _():
        carry_ref[...] = jnp.zeros_like(carry_ref)

    x = x_ref[...]
    D = x.shape[1]
    hn = _rms(x, nm_ref[...]).astype(BF16)
    gates = jnp.dot(hn, wg_ref[...], preferred_element_type=F32)
    pool_proj = jnp.dot(po_ref[...], wpo_ref[...], preferred_element_type=F32)
    sb_proj = jnp.dot(sb_ref[...], wsb_ref[...], preferred_element_type=F32)
    merged = jax.nn.sigmoid(gates[:, :D]) * pool_proj + jax.nn.sigmoid(gates[:, D:]) * sb_proj
    h2 = x + jnp.dot(merged.astype(BF16), wo_ref[...], preferred_element_type=F32)
    h2_ref[...] = h2
    hn2 = _rms(h2, nf_ref[...])
    _store_rows(hn2_ref, hn2)

    logits = jnp.dot(hn2.astype(BF16), wr_ref[...], preferred_element_type=F32) + br_ref[...]
    lane = lax.broadcasted_iota(jnp.int32, logits.shape, 1)
    lanef = lane.astype(F32)
    neg = jnp.float32(-jnp.inf)
    big = jnp.float32(2 * ROUTER_LANES)

    def first_argmax(vals):
        m = jnp.max(vals, axis=1, keepdims=True)
        return m, jnp.min(jnp.where(vals == m, lanef, big), axis=1, keepdims=True)

    is_group = lane < N_GROUPS
    gmax, gidx = first_argmax(jnp.where(is_group, logits, neg))
    p_group = 1.0 / jnp.sum(jnp.where(is_group, jnp.exp(logits - gmax), 0.0), axis=1, keepdims=True)
    first = N_GROUPS + EXPERTS_PER_GROUP * gidx
    in_group = (lanef >= first) & (lanef < first + EXPERTS_PER_GROUP)
    cand = jnp.where(in_group, logits, neg)
    m1, i1 = first_argmax(cand)
    m2, i2 = first_argmax(jnp.where(lanef == i1, neg, cand))
    t = jnp.exp(m2 - m1)
    w1 = p_group / (1.0 + t)
    w2 = p_group * t / (1.0 + t)

    hit1 = lanef == i1
    hit2 = lanef == i2
    onehot = jnp.where(hit1 | hit2, 1.0, 0.0)
    before = jnp.dot(low_ref[...], onehot.astype(BF16), preferred_element_type=F32) + carry_ref[...]
    r1 = jnp.sum(jnp.where(hit1, before, 0.0), axis=1, keepdims=True)
    r2 = jnp.sum(jnp.where(hit2, before, 0.0), axis=1, keepdims=True)
    carry_ref[...] = carry_ref[...] + jnp.sum(onehot, axis=0, keepdims=True)
    cnt_ref[...] = carry_ref[...]

    slab = jnp.where(lane == 0, i1 - N_GROUPS,
                     jnp.where(lane == 1, i2 - N_GROUPS, jnp.where(lane == 2, r1, jnp.where(lane == 3, r2, 0.0))))
    ri_ref[...] = slab.astype(jnp.int32)
    rw_ref[...] = jnp.where(lane == 0, w1, jnp.where(lane == 1, w2, 0.0))


def _merge_route(x, po, sb, norm_mix, w_gates, w_pool_out, w_sb_out, w_o, norm_ffn, w_router, b_router, lower):
    N, D = x.shape
    tm = lower.shape[0]
    assert N % tm == 0
    row = lambda i: (i, 0)
    const = lambda i: (0, 0)
    full = lambda a: pl.BlockSpec(a.shape, const)
    return pl.pallas_call(
        functools.partial(_merge_body, tm=tm),
        grid=(N // tm,),
        in_specs=[pl.BlockSpec((tm, D), row), pl.BlockSpec((tm, POOL_WIDTH), row), pl.BlockSpec((tm, SB_WIDTH), row),
                  full(norm_mix), full(w_gates), full(w_pool_out), full(w_sb_out), full(w_o), full(norm_ffn),
                  full(w_router), full(b_router), full(lower)],
        out_specs=[pl.BlockSpec((tm, D), row), pl.BlockSpec((tm, D // ROW_LANES, ROW_LANES), lambda i: (i, 0, 0)),
                   pl.BlockSpec((tm, ROUTER_LANES), row), pl.BlockSpec((tm, ROUTER_LANES), row),
                   pl.BlockSpec((1, ROUTER_LANES), const)],
        out_shape=(jax.ShapeDtypeStruct((N, D), F32), jax.ShapeDtypeStruct((N, D // ROW_LANES, ROW_LANES), F32),
                   jax.ShapeDtypeStruct((N, ROUTER_LANES), jnp.int32), jax.ShapeDtypeStruct((N, ROUTER_LANES), F32),
                   jax.ShapeDtypeStruct((1, ROUTER_LANES), F32)),
        scratch_shapes=[pltpu.VMEM((1, ROUTER_LANES), F32)],
        compiler_params=_params("arbitrary"),
        name="merge_route",
    )(x, po, sb, norm_mix, w_gates, w_pool_out, w_sb_out, w_o, norm_ffn, w_router, b_router, lower)


def _row_copy(src_ref, src_row, dst_ref, dst_row, sem):
    return pltpu.make_async_copy(src_ref.at[src_row], dst_ref.at[dst_row], sem)


def _dispatch_body(pend_ref, padded_ref, dest_ref, x_ref, buf_ref, zero_ref, sem, zsem, *, tm, bm):
    @pl.when(pl.program_id(0) == 0)
    def _():
        zero_ref[...] = jnp.zeros_like(zero_ref)
        for e in range(N_EXPERTS):
            @pl.when(padded_ref[e] > 0)
            def _():
                cp = pltpu.make_async_copy(zero_ref, buf_ref.at[pl.ds(pend_ref[e] - bm, bm)], zsem)
                cp.start()
                cp.wait()

        def zero_unused_block(bi, c):
            cp = pltpu.make_async_copy(zero_ref, buf_ref.at[pl.ds(bi * bm, bm)], zsem)
            cp.start()
            cp.wait()
            return c

        lax.fori_loop(lax.div(pend_ref[N_EXPERTS - 1], bm), buf_ref.shape[0] // bm, zero_unused_block, 0)

    def issue(r, c):
        _row_copy(x_ref, r, buf_ref, dest_ref[0, 2 * r], sem).start()
        _row_copy(x_ref, r, buf_ref, dest_ref[0, 2 * r + 1], sem).start()
        return c

    lax.fori_loop(0, tm, issue, 0)

    def drain(r, c):
        _row_copy(x_ref, 0, buf_ref, 0, sem).wait()
        _row_copy(x_ref, 0, buf_ref, 0, sem).wait()
        return c

    lax.fori_loop(0, tm, drain, 0)


def _dispatch(hn2, dest, pend, padded, cap, bm, tm):
    N, C, W = hn2.shape
    grid_spec = pltpu.PrefetchScalarGridSpec(
        num_scalar_prefetch=2,
        grid=(N // tm,),
        in_specs=[pl.BlockSpec((None, 1, 2 * tm), lambda i, pe, pa: (i, 0, 0), memory_space=pltpu.SMEM),
                  pl.BlockSpec((tm, C, W), lambda i, pe, pa: (i, 0, 0))],
        out_specs=pl.BlockSpec(memory_space=pl.ANY),
        scratch_shapes=[pltpu.VMEM((bm, C, W), F32), pltpu.SemaphoreType.DMA, pltpu.SemaphoreType.DMA],
    )
    return pl.pallas_call(
        functools.partial(_dispatch_body, tm=tm, bm=bm),
        grid_spec=grid_spec,
        out_shape=jax.ShapeDtypeStruct((cap, C, W), F32),
        compiler_params=_params("arbitrary"),
        name="dispatch",
    )(pend, padded, dest.reshape(N // tm, 1, 2 * tm), hn2)


def _expert_body(blk_e_ref, nact_ref, x_ref, wg_ref, wu_ref, wd_ref, y_ref):
    @pl.when(pl.program_id(0) < nact_ref[0])
    def _():
        x = _load_rows(x_ref).astype(BF16)
        gate = jnp.dot(x, wg_ref[...], preferred_element_type=F32)
        up = jnp.dot(x, wu_ref[...], preferred_element_type=F32)
        hidden = (jax.nn.silu(gate) * up).astype(BF16)
        _store_rows(y_ref, jnp.dot(hidden, wd_ref[...], preferred_element_type=F32))

    @pl.when(pl.program_id(0) >= nact_ref[0])
    def _():
        y_ref[...] = jnp.zeros_like(y_ref)


def _experts(buf, blk_e, nact, w_gate, w_up, w_down, bm):
    cap, C, W = buf.shape
    D, DE = w_gate.shape[1], w_gate.shape[2]
    rows = lambda i, be, na: (jnp.minimum(i, na[0] - 1), 0, 0)
    grid_spec = pltpu.PrefetchScalarGridSpec(
        num_scalar_prefetch=2,
        grid=(cap // bm,),
        in_specs=[pl.BlockSpec((bm, C, W), rows),
                  pl.BlockSpec((None, D, DE), lambda i, be, na: (be[i], 0, 0)),
                  pl.BlockSpec((None, D, DE), lambda i, be, na: (be[i], 0, 0)),
                  pl.BlockSpec((None, DE, D), lambda i, be, na: (be[i], 0, 0))],
        out_specs=pl.BlockSpec((bm, C, W), lambda i, be, na: (i, 0, 0)),
    )
    return pl.pallas_call(
        _expert_body,
        grid_spec=grid_spec,
        out_shape=jax.ShapeDtypeStruct((cap, C, W), F32),
        compiler_params=_params("arbitrary"),
        name="experts",
    )(blk_e, nact, buf, w_gate, w_up, w_down)


def _combine_body(dest_ref, h2_ref, rw_ref, yb_ref, y_ref, rows_ref, sem, *, tm):
    def issue(r, c):
        _row_copy(yb_ref, dest_ref[0, 2 * r], rows_ref.at[0], r, sem).start()
        _row_copy(yb_ref, dest_ref[0, 2 * r + 1], rows_ref.at[1], r, sem).start()
        return c

    lax.fori_loop(0, tm, issue, 0)

    def drain(r, c):
        _row_copy(yb_ref, 0, rows_ref.at[0], 0, sem).wait()
        _row_copy(yb_ref, 0, rows_ref.at[1], 0, sem).wait()
        return c

    lax.fori_loop(0, tm, drain, 0)
    w = rw_ref[...]
    y_ref[...] = h2_ref[...] + (_load_rows(rows_ref.at[0]) * w[:, 0:1] + _load_rows(rows_ref.at[1]) * w[:, 1:2])


def _combine(h2, rw, dest, yb, tm):
    N, D = h2.shape
    C, W = yb.shape[1:]
    return pl.pallas_call(
        functools.partial(_combine_body, tm=tm),
        grid=(N // tm,),
        in_specs=[pl.BlockSpec((None, 1, 2 * tm), lambda i: (i, 0, 0), memory_space=pltpu.SMEM),
                  pl.BlockSpec((tm, D), lambda i: (i, 0)),
                  pl.BlockSpec((tm, ROUTER_LANES), lambda i: (i, 0)),
                  pl.BlockSpec(memory_space=pl.ANY)],
        out_specs=pl.BlockSpec((tm, D), lambda i: (i, 0)),
        out_shape=jax.ShapeDtypeStruct((N, D), F32),
        scratch_shapes=[pltpu.VMEM((2, tm, C, W), F32), pltpu.SemaphoreType.DMA],
        compiler_params=_params("arbitrary"),
        name="combine",
    )(dest.reshape(N // tm, 1, 2 * tm), h2, rw, yb)


def _moe(h2, hn2, ri, rw, counts, w_gate, w_up, w_down, tm, bm):
    N, D = h2.shape
    cnt = counts[0, N_GROUPS:N_GROUPS + N_EXPERTS].astype(jnp.int32)
    padded = (cnt + bm - 1) // bm * bm
    pend = jnp.cumsum(padded)
    pstart = pend - padded
    experts, rank = ri[:, 0:2], ri[:, 2:4]
    dest = (pstart[experts] + rank).astype(jnp.int32)
    cap = -(-(2 * N + N_EXPERTS * (bm - 1)) // bm) * bm
    nblk = cap // bm
    starts = jnp.arange(nblk, dtype=jnp.int32)[:, None] * bm
    blk_e = jnp.minimum(jnp.sum((pend[None, :] <= starts).astype(jnp.int32), axis=1), N_EXPERTS - 1)
    nact = (pend[-1:] // bm).astype(jnp.int32)
    buf = _dispatch(hn2, dest, pend.astype(jnp.int32), padded.astype(jnp.int32), cap, bm, tm)
    yb = _experts(buf, blk_e, nact, w_gate, w_up, w_down, bm)
    return _combine(h2, rw, dest, yb, tm)


def _constants(tm):
    r = lax.broadcasted_iota(jnp.int32, (KEY_BLOCK, KEY_BLOCK), 0)
    c = lax.broadcasted_iota(jnp.int32, (KEY_BLOCK, KEY_BLOCK), 1)
    tri = jnp.where(r > c, -1.0, 0.0).astype(BF16)
    hr = lax.broadcasted_iota(jnp.int32, (SB_WIDTH, SB_WIDTH), 0) // SB_HEAD_DIM
    hc = lax.broadcasted_iota(jnp.int32, (SB_WIDTH, SB_WIDTH), 1) // SB_HEAD_DIM
    hsum = jnp.where(hr == hc, 1.0 / SB_HEAD_DIM, 0.0).astype(BF16)
    lr = lax.broadcasted_iota(jnp.int32, (tm, tm), 0)
    lc = lax.broadcasted_iota(jnp.int32, (tm, tm), 1)
    lower = jnp.where(lc < lr, 1.0, 0.0).astype(BF16)
    return tri, hsum, lower


def kernel(x_prompt, x_sample, cache_sb_k, cache_sb_v, state_pool, meta_tokens, norm_mix, w_in, q_norm, k_norm,
           pool_mix, pool_scale, w_pool_out, w_sb_out, w_o, norm_ffn, w_router_group, b_router_group,
           w_router_expert, b_router_expert, w_gate, w_up, w_down):
    assert norm_mix.shape[0] == 1, "single-layer trunk"
    B, S, D = x_prompt.shape
    BS, SS, _ = x_sample.shape
    past = cache_sb_k.shape[2]
    tm_p, tm_s = min(512, B * S), min(512, BS * SS)
    tri, hsum, lower_p = _constants(tm_p)
    lower_s = lower_p if tm_s == tm_p else _constants(tm_s)[2]

    w_in0 = w_in[0]
    w_uqkv = w_in0[:, :4 * SB_WIDTH].astype(BF16)
    w_gates = w_in0[:, 4 * SB_WIDTH:].astype(BF16)
    nm = norm_mix[0][None, :]
    nf = norm_ffn[0][None, :]
    qg = jnp.tile(q_norm[0], SB_HEADS)[None, :]
    kg = jnp.tile(k_norm[0], SB_HEADS)[None, :]
    pm = pool_mix[0].astype(BF16)
    ps = pool_scale[0][None, :]
    wpo = w_pool_out[0].astype(BF16)
    wsb = w_sb_out[0].astype(BF16)
    wo = w_o[0].astype(BF16)
    pad = ROUTER_LANES - N_GROUPS - N_EXPERTS
    w_router = jnp.concatenate([w_router_group[0], w_router_expert[0], jnp.zeros((D, pad), F32)], axis=1).astype(BF16)
    b_router = jnp.concatenate([b_router_group[0], b_router_expert[0], jnp.zeros((pad,), F32)])[None, :]
    wg_e, wu_e, wd_e = w_gate[0].astype(BF16), w_up[0].astype(BF16), w_down[0].astype(BF16)

    inproj = functools.partial(_inproj, norm_mix=nm, w_uqkv=w_uqkv, qg=qg, kg=kg, hsum=hsum, pool_mix=pm,
                               pool_scale=ps)

    def tail(x, po, sb, lower, bm):
        n = x.shape[0] * x.shape[1]
        h2, hn2, ri, rw, counts = _merge_route(x.reshape(n, D), po.reshape(n, POOL_WIDTH), sb.reshape(n, SB_WIDTH),
                                               nm, w_gates, wpo, wsb, wo, nf, w_router, b_router, lower)
        return _moe(h2, hn2, ri, rw, counts, wg_e, wu_e, wd_e, lower.shape[0], bm).reshape(x.shape)

    shape5 = lambda a: a.reshape(1, a.shape[0], a.shape[1], SB_HEADS, SB_HEAD_DIM)

    prev_s = jnp.concatenate([jnp.zeros((BS, CARRY_ROWS - POOL_STATE, POOL_WIDTH), F32), state_pool[0]], axis=1)
    q_s, k_s, v_s, ks_f, vs_f, po_s, utail_s = inproj(x_sample, prev_s)
    kc = cache_sb_k[0].reshape(BS, past, SB_WIDTH)
    vc = cache_sb_v[0].reshape(BS, past, SB_WIDTH)
    sb_s = _attention(q_s, k_s, v_s, kc, vc, tri)
    y_sample = tail(x_sample, po_s, sb_s, lower_s, 128)
    new_pool_sample = utail_s[None, :, CARRY_ROWS - POOL_STATE:, :]

    zeros_prev = jnp.zeros((1, CARRY_ROWS, POOL_WIDTH), F32)
    _, km_b, vm_b, km_f, vm_f, _, u_meta = inproj(meta_tokens[None], zeros_prev)

    q_b, k_b, v_b, k_f, v_f, po, utail = inproj(x_prompt, u_meta)
    sb = _attention(q_b, k_b, v_b, km_b, vm_b, tri)
    y_prompt = tail(x_prompt, po, sb, lower_p, 512)
    new_k_prompt = shape5(jnp.concatenate([jnp.broadcast_to(km_f, (B, N_META, SB_WIDTH)), k_f], axis=1))
    new_v_prompt = shape5(jnp.concatenate([jnp.broadcast_to(vm_f, (B, N_META, SB_WIDTH)), v_f], axis=1))
    new_pool_prompt = utail[None, :, CARRY_ROWS - POOL_STATE:, :]

    return (y_prompt, y_sample, new_k_prompt, new_v_prompt, new_pool_prompt,
            shape5(ks_f), shape5(vs_f), new_pool_sample)
```

```python
import functools

import jax
import jax.numpy as jnp
from jax import lax
from jax.experimental import pallas as pl
from jax.experimental.pallas import tpu as pltpu

F32 = jnp.float32
BF16 = jnp.bfloat16

N_META = 16
POOL_WINDOWS = (2, 4, 8, 16)
POOL_GROUP = 128
POOL_WIDTH = 512
POOL_STATE = 15
SB_HEADS = 8
SB_HEAD_DIM = 64
SB_WIDTH = 512
HEAD_PAIR_LANES = 2 * SB_HEAD_DIM
N_GROUPS = 4
EXPERTS_PER_GROUP = 8
N_EXPERTS = 32
ROUTER_LANES = 128
ROW_TILE = (8, 128)
ROW_DMA_UNROLL = 8
NORM_EPS = 1e-6
CARRY_ROWS = 16
KEY_BLOCK = 256
VMEM_LIMIT_BYTES = 56 * 1024 * 1024


def _params(*sem):
    return pltpu.CompilerParams(dimension_semantics=sem, vmem_limit_bytes=VMEM_LIMIT_BYTES)


def _rms(x, gain):
    ms = jnp.mean(x * x, axis=-1, keepdims=True)
    return x * lax.rsqrt(ms + NORM_EPS) * gain


def _inproj_body(x_ref, nm_ref, w_ref, qg_ref, kg_ref, hs_ref, pm_ref, ps_ref, uprev_ref,
                 qb_ref, kb_ref, vb_ref, kf_ref, vf_ref, po_ref, ut_ref, carry_ref, *, tm):
    @pl.when(pl.program_id(1) == 0)
    def _():
        carry_ref[...] = uprev_ref[...]

    hn = _rms(x_ref[...], nm_ref[...]).astype(BF16)
    z = jnp.dot(hn, w_ref[...], preferred_element_type=F32)
    u = z[:, 0:POOL_WIDTH]
    v = z[:, 3 * SB_WIDTH:4 * SB_WIDTH]

    def head_norm(a, gain):
        ms = jnp.dot((a * a).astype(BF16), hs_ref[...], preferred_element_type=F32)
        return a * lax.rsqrt(ms + NORM_EPS) * gain

    qn = head_norm(z[:, SB_WIDTH:2 * SB_WIDTH], qg_ref[...])
    kn = head_norm(z[:, 2 * SB_WIDTH:3 * SB_WIDTH], kg_ref[...])
    qb_ref[...] = (qn * (SB_HEAD_DIM ** -0.5)).astype(BF16)
    kb_ref[...] = kn.astype(BF16)
    vb_ref[...] = v.astype(BF16)
    kf_ref[...] = kn
    vf_ref[...] = v

    ext = jnp.concatenate([carry_ref[...], u], axis=0)
    mixed = []
    for gi, w in enumerate(POOL_WINDOWS):
        c0, c1 = gi * POOL_GROUP, (gi + 1) * POOL_GROUP
        s = ext[:, c0:c1]
        shift = 1
        while shift < w:
            s = s + pltpu.roll(s, shift, axis=0)
            shift *= 2
        pooled = s[CARRY_ROWS:, :] * (1.0 / w) - u[:, c0:c1]
        mixed.append(jnp.dot(pooled.astype(BF16), pm_ref[gi], preferred_element_type=F32))
    po_ref[...] = (jnp.concatenate(mixed, axis=1) * ps_ref[...]).astype(BF16)
    carry_ref[...] = u[tm - CARRY_ROWS:, :]
    ut_ref[...] = u[tm - CARRY_ROWS:, :]


def _inproj(x, uprev, norm_mix, w_uqkv, qg, kg, hsum, pool_mix, pool_scale):
    B, S, D = x.shape
    tm = min(512, S)
    assert S % tm == 0 and tm % CARRY_ROWS == 0
    per_batch_prev = uprev.shape[0] == B and B > 1
    row = lambda b, t: (b, t, 0)
    const2 = lambda b, t: (0, 0)
    out_shape = (
        jax.ShapeDtypeStruct((B, S, SB_WIDTH), BF16),
        jax.ShapeDtypeStruct((B, S, SB_WIDTH), BF16),
        jax.ShapeDtypeStruct((B, S, SB_WIDTH), BF16),
        jax.ShapeDtypeStruct((B, S, SB_WIDTH), F32),
        jax.ShapeDtypeStruct((B, S, SB_WIDTH), F32),
        jax.ShapeDtypeStruct((B, S, POOL_WIDTH), BF16),
        jax.ShapeDtypeStruct((B, CARRY_ROWS, POOL_WIDTH), F32),
    )
    blk = lambda w: pl.BlockSpec((None, tm, w), row)
    return pl.pallas_call(
        functools.partial(_inproj_body, tm=tm),
        grid=(B, S // tm),
        in_specs=[
            pl.BlockSpec((None, tm, D), row),
            pl.BlockSpec((1, D), const2),
            pl.BlockSpec((D, 4 * SB_WIDTH), const2),
            pl.BlockSpec((1, SB_WIDTH), const2),
            pl.BlockSpec((1, SB_WIDTH), const2),
            pl.BlockSpec((SB_WIDTH, SB_WIDTH), const2),
            pl.BlockSpec((len(POOL_WINDOWS), POOL_GROUP, POOL_GROUP), lambda b, t: (0, 0, 0)),
            pl.BlockSpec((1, POOL_WIDTH), const2),
            pl.BlockSpec((None, CARRY_ROWS, POOL_WIDTH),
                         (lambda b, t: (b, 0, 0)) if per_batch_prev else (lambda b, t: (0, 0, 0))),
        ],
        out_specs=[blk(SB_WIDTH), blk(SB_WIDTH), blk(SB_WIDTH), blk(SB_WIDTH), blk(SB_WIDTH), blk(POOL_WIDTH),
                   pl.BlockSpec((None, CARRY_ROWS, POOL_WIDTH), lambda b, t: (b, 0, 0))],
        out_shape=out_shape,
        scratch_shapes=[pltpu.VMEM((CARRY_ROWS, POOL_WIDTH), F32)],
        compiler_params=_params("arbitrary", "arbitrary"),
        name="inproj",
    )(x, norm_mix, w_uqkv, qg, kg, hsum, pool_mix, pool_scale, uprev)


def _log_gates(z):
    sign = jnp.uint32(0x80000000)
    neg_abs = lax.bitcast_convert_type(lax.bitcast_convert_type(z, jnp.uint32) | sign, F32)
    sp = jnp.maximum(z, 0.0) + jnp.log(1.0 + jnp.exp(neg_abs))
    return sp, z - sp


def _dot_nt(a, b):
    return lax.dot_general(a, b, (((1,), (1,)), ((), ())), preferred_element_type=F32)


def _attn_body(vq_ref, vk_ref, q_ref, k_ref, v_ref, kp_ref, vp_ref, tri_ref, o_ref,
               z_scr, e_scr, acc_scr, run_scr, *, L, P, tq, n_visits, full_from_prefix):
    nq = L // tq
    lane = lax.broadcasted_iota(jnp.int32, (1, HEAD_PAIR_LANES), 1)
    heads = (lane < SB_HEAD_DIM, lane >= SB_HEAD_DIM)
    ksrc, vsrc = (kp_ref, vp_ref) if full_from_prefix else (k_ref, v_ref)

    def stacked_q(qi):
        qp = q_ref[pl.ds(pl.multiple_of(qi * tq, tq), tq), :]
        zero = jnp.zeros((), BF16)
        return jnp.concatenate([jnp.where(heads[0], qp, zero), jnp.where(heads[1], qp, zero)], axis=0)

    def key_rows(s):
        return pl.ds(pl.multiple_of(vk_ref[s] * KEY_BLOCK, KEY_BLOCK), KEY_BLOCK)

    def stage_a(s):
        z_scr[...] = _dot_nt(stacked_q(vq_ref[s]), ksrc[key_rows(s), :].astype(BF16))

    def stage_b(s):
        qi = vq_ref[s]
        sp, ls = _log_gates(z_scr[...])
        later = jnp.dot(sp.astype(BF16), tri_ref[...], preferred_element_type=F32)
        run = run_scr[qi]
        e_scr[...] = ls + (later + jnp.concatenate([run, run], axis=1))
        run_scr[qi] = run - jnp.sum(sp, axis=1, keepdims=True)

    def stage_c(s):
        qi = vq_ref[s]
        a = jnp.exp(e_scr[...]).astype(BF16)
        acc_scr[qi] = acc_scr[qi] + jnp.dot(a, vsrc[key_rows(s), :].astype(BF16), preferred_element_type=F32)

    if n_visits:
        acc_scr[...] = jnp.zeros_like(acc_scr)
        run_scr[...] = jnp.zeros_like(run_scr)
        stage_a(0)
        if n_visits >= 2:
            stage_b(0)
            stage_a(1)

        def steady(s, c):
            stage_c(s - 2)
            stage_b(s - 1)
            stage_a(s)
            return c

        lax.fori_loop(2, n_visits, steady, 0)
        if n_visits >= 2:
            stage_c(n_visits - 2)
        stage_b(n_visits - 1)
        stage_c(n_visits - 1)

    rows = lax.broadcasted_iota(jnp.int32, (2 * tq, tq), 0)
    cols = lax.broadcasted_iota(jnp.int32, (2 * tq, tq), 1)
    visible = (cols < rows) & (rows < tq) | (cols < rows - tq)

    def q_block(qi):
        qrows = pl.ds(pl.multiple_of(qi * tq, tq), tq)
        q2 = stacked_q(qi)
        sp, ls = _log_gates(_dot_nt(q2, k_ref[qrows, :]))
        sp = jnp.where(visible, sp, 0.0)
        later = jnp.dot(sp.astype(BF16), tri_ref[0:tq, 0:tq], preferred_element_type=F32)
        a = jnp.where(visible, jnp.exp(ls + later), 0.0)
        acc = jnp.dot(a.astype(BF16), v_ref[qrows, :], preferred_element_type=F32)
        run = -jnp.sum(sp, axis=1, keepdims=True)
        if n_visits:
            acc = acc + jnp.exp(run) * acc_scr[qi]
            run = run + run_scr[qi][:, 0:1]
        if not full_from_prefix:
            sp, ls = _log_gates(_dot_nt(q2, kp_ref[...].astype(BF16)))
            later = jnp.dot(sp.astype(BF16), tri_ref[0:P, 0:P], preferred_element_type=F32)
            a = jnp.exp(ls + (later + run))
            acc = acc + jnp.dot(a.astype(BF16), vp_ref[...].astype(BF16), preferred_element_type=F32)
        o_ref[qrows, :] = jnp.where(heads[0], acc[:tq], acc[tq:]).astype(BF16)

    def q_block_pair(j, carry):
        q_block(2 * j)
        q_block(2 * j + 1)
        return carry

    lax.fori_loop(0, nq // 2, q_block_pair, 0)
    if nq % 2:
        q_block(nq - 1)


def _attention(q, k, v, kp, vp, tri):
    B, L, _ = q.shape
    P = kp.shape[1]
    tq = min(KEY_BLOCK, L)
    nq = L // tq
    assert L % tq == 0 and tq % 16 == 0 and P % 16 == 0
    full_from_prefix = P > KEY_BLOCK
    if full_from_prefix:
        assert nq == 1 and P % KEY_BLOCK == 0
        visits = [(qi, kb) for qi in range(nq) for kb in reversed(range(P // KEY_BLOCK))]
    else:
        assert tq == KEY_BLOCK or nq == 1
        visits = [(qi, kb) for qi in range(nq) for kb in reversed(range(qi))]
    n_visits = len(visits)
    vq = jnp.asarray([qi for qi, _ in visits] or [0], jnp.int32)
    vk = jnp.asarray([kb for _, kb in visits] or [0], jnp.int32)
    per_batch_prefix = kp.shape[0] == B and B > 1
    seq = pl.BlockSpec((None, L, HEAD_PAIR_LANES), lambda b, hp, *_: (b, 0, hp))
    pre = pl.BlockSpec((None, P, HEAD_PAIR_LANES),
                       (lambda b, hp, *_: (b, 0, hp)) if per_batch_prefix else (lambda b, hp, *_: (0, 0, hp)))
    grid_spec = pltpu.PrefetchScalarGridSpec(
        num_scalar_prefetch=2,
        grid=(B, SB_WIDTH // HEAD_PAIR_LANES),
        in_specs=[seq, seq, seq, pre, pre, pl.BlockSpec((KEY_BLOCK, KEY_BLOCK), lambda b, hp, *_: (0, 0))],
        out_specs=seq,
        scratch_shapes=[pltpu.VMEM((2 * tq, KEY_BLOCK), F32), pltpu.VMEM((2 * tq, KEY_BLOCK), F32),
                        pltpu.VMEM((nq, 2 * tq, HEAD_PAIR_LANES), F32),
                        pltpu.VMEM((nq, 2 * tq, HEAD_PAIR_LANES), F32)],
    )
    return pl.pallas_call(
        functools.partial(_attn_body, L=L, P=P, tq=tq, n_visits=n_visits, full_from_prefix=full_from_prefix),
        grid_spec=grid_spec,
        out_shape=jax.ShapeDtypeStruct((B, L, SB_WIDTH), BF16),
        compiler_params=_params("arbitrary", "arbitrary"),
        name="stickbreak",
    )(vq, vk, q, k, v, kp, vp, tri)


def _transpose8(v):
    sub = lax.broadcasted_iota(jnp.int32, (1, ROW_TILE[0], 1), 1)
    for k in (4, 2, 1):
        low = (sub & k) == 0
        nxt = list(v)
        for i in range(ROW_TILE[0]):
            if i & k == 0:
                j = i | k
                nxt[i] = jnp.where(low, v[i], pltpu.roll(v[j], k, axis=1))
                nxt[j] = jnp.where(low, pltpu.roll(v[i], ROW_TILE[0] - k, axis=1), v[j])
        v = nxt
    return v


def _store_rows(ref, val):
    s, w = ROW_TILE
    g = val.shape[0] // s
    v = _transpose8([val[:, c * w:(c + 1) * w].reshape(g, s, w) for c in range(s)])
    for r in range(s):
        ref[pl.ds(r, g, stride=s), :, :] = v[r]


def _load_rows(ref):
    s, w = ROW_TILE
    g = ref.shape[0] // s
    v = _transpose8([ref[pl.ds(r, g, stride=s), :, :] for r in range(s)])
    return jnp.concatenate([x.reshape(g * s, w) for x in v], axis=1)


def _merge_body(x_ref, po_ref, sb_ref, nm_ref, wg_ref, wpo_ref, wsb_ref, wo_ref, nf_ref, wr_ref, br_ref, low_ref,
                h2_ref, hn2_ref, ri_ref, rw_ref, cnt_ref, carry_ref, *, tm):
    @pl.when(pl.program_id(0) == 0)
    def _():
        carry_ref[...] = jnp.zeros_like(carry_ref)

    x = x_ref[...]
    D = x.shape[1]
    hn = _rms(x, nm_ref[...]).astype(BF16)
    gates = jnp.dot(hn, wg_ref[...], preferred_element_type=F32)
    pool_proj = jnp.dot(po_ref[...], wpo_ref[...], preferred_element_type=F32)
    sb_proj = jnp.dot(sb_ref[...], wsb_ref[...], preferred_element_type=F32)
    merged = jax.nn.sigmoid(gates[:, :D]) * pool_proj + jax.nn.sigmoid(gates[:, D:]) * sb_proj
    h2 = x + jnp.dot(merged.astype(BF16), wo_ref[...], preferred_element_type=F32)
    h2_ref[...] = h2
    hn2 = _rms(h2, nf_ref[...])
    _store_rows(hn2_ref, hn2)

    logits = jnp.dot(hn2.astype(BF16), wr_ref[...], preferred_element_type=F32) + br_ref[...]
    lane = lax.broadcasted_iota(jnp.int32, logits.shape, 1)
    lanef = lane.astype(F32)
    neg = jnp.float32(-jnp.inf)
    big = jnp.float32(2 * ROUTER_LANES)

    def first_argmax(vals):
        m = jnp.max(vals, axis=1, keepdims=True)
        return m, jnp.min(jnp.where(vals == m, lanef, big), axis=1, keepdims=True)

    is_group = lane < N_GROUPS
    gmax, gidx = first_argmax(jnp.where(is_group, logits, neg))
    p_group = 1.0 / jnp.sum(jnp.where(is_group, jnp.exp(logits - gmax), 0.0), axis=1, keepdims=True)
    first = N_GROUPS + EXPERTS_PER_GROUP * gidx
    in_group = (lanef >= first) & (lanef < first + EXPERTS_PER_GROUP)
    cand = jnp.where(in_group, logits, neg)
    m1, i1 = first_argmax(cand)
    m2, i2 = first_argmax(jnp.where(lanef == i1, neg, cand))
    t = jnp.exp(m2 - m1)
    w1 = p_group / (1.0 + t)
    w2 = p_group * t / (1.0 + t)

    hit1 = lanef == i1
    hit2 = lanef == i2
    onehot = jnp.where(hit1 | hit2, 1.0, 0.0)
    before = jnp.dot(low_ref[...], onehot.astype(BF16), preferred_element_type=F32) + carry_ref[...]
    r1 = jnp.sum(jnp.where(hit1, before, 0.0), axis=1, keepdims=True)
    r2 = jnp.sum(jnp.where(hit2, before, 0.0), axis=1, keepdims=True)
    carry_ref[...] = carry_ref[...] + jnp.sum(onehot, axis=0, keepdims=True)
    cnt_ref[...] = carry_ref[...]

    slab = jnp.where(lane == 0, i1 - N_GROUPS,
                     jnp.where(lane == 1, i2 - N_GROUPS, jnp.where(lane == 2, r1, jnp.where(lane == 3, r2, 0.0))))
    ri_ref[...] = slab.astype(jnp.int32)
    rw_ref[...] = jnp.where(lane == 0, w1, jnp.where(lane == 1, w2, 0.0))


def _merge_route(x, po, sb, norm_mix, w_gates, w_pool_out, w_sb_out, w_o, norm_ffn, w_router, b_router, lower):
    N, D = x.shape
    tm = lower.shape[0]
    assert N % tm == 0
    row = lambda i: (i, 0)
    const = lambda i: (0, 0)
    full = lambda a: pl.BlockSpec(a.shape, const)
    return pl.pallas_call(
        functools.partial(_merge_body, tm=tm),
        grid=(N // tm,),
        in_specs=[pl.BlockSpec((tm, D), row), pl.BlockSpec((tm, POOL_WIDTH), row), pl.BlockSpec((tm, SB_WIDTH), row),
                  full(norm_mix), full(w_gates), full(w_pool_out), full(w_sb_out), full(w_o), full(norm_ffn),
                  full(w_router), full(b_router), full(lower)],
        out_specs=[pl.BlockSpec((tm, D), row), pl.BlockSpec((tm,) + ROW_TILE, lambda i: (i, 0, 0)),
                   pl.BlockSpec((tm, ROUTER_LANES), row), pl.BlockSpec((tm, ROUTER_LANES), row),
                   pl.BlockSpec((1, ROUTER_LANES), const)],
        out_shape=(jax.ShapeDtypeStruct((N, D), F32), jax.ShapeDtypeStruct((N,) + ROW_TILE, F32),
                   jax.ShapeDtypeStruct((N, ROUTER_LANES), jnp.int32), jax.ShapeDtypeStruct((N, ROUTER_LANES), F32),
                   jax.ShapeDtypeStruct((1, ROUTER_LANES), F32)),
        scratch_shapes=[pltpu.VMEM((1, ROUTER_LANES), F32)],
        compiler_params=_params("arbitrary"),
        name="merge_route",
    )(x, po, sb, norm_mix, w_gates, w_pool_out, w_sb_out, w_o, norm_ffn, w_router, b_router, lower)


def _row_copy(src_ref, src_row, dst_ref, dst_row, sem):
    return pltpu.make_async_copy(src_ref.at[src_row], dst_ref.at[dst_row], sem)


def _dispatch_body(pend_ref, padded_ref, dest_ref, x_ref, buf_ref, zero_ref, sem, zsem, *, tm, bm):
    @pl.when(pl.program_id(0) == 0)
    def _():
        zero_ref[...] = jnp.zeros_like(zero_ref)
        for e in range(N_EXPERTS):
            @pl.when(padded_ref[e] > 0)
            def _():
                cp = pltpu.make_async_copy(zero_ref, buf_ref.at[pl.ds(pend_ref[e] - bm, bm)], zsem)
                cp.start()
                cp.wait()

        def zero_unused_block(bi, c):
            cp = pltpu.make_async_copy(zero_ref, buf_ref.at[pl.ds(bi * bm, bm)], zsem)
            cp.start()
            cp.wait()
            return c

        lax.fori_loop(lax.div(pend_ref[N_EXPERTS - 1], bm), buf_ref.shape[0] // bm, zero_unused_block, 0)

    def issue(j, c):
        for u in range(ROW_DMA_UNROLL):
            r = j * ROW_DMA_UNROLL + u
            _row_copy(x_ref, r, buf_ref, dest_ref[0, 2 * r], sem).start()
            _row_copy(x_ref, r, buf_ref, dest_ref[0, 2 * r + 1], sem).start()
        return c

    lax.fori_loop(0, tm // ROW_DMA_UNROLL, issue, 0)

    def drain(j, c):
        for _ in range(2 * ROW_DMA_UNROLL):
            _row_copy(x_ref, 0, buf_ref, 0, sem).wait()
        return c

    lax.fori_loop(0, tm // ROW_DMA_UNROLL, drain, 0)


def _dispatch(hn2, dest, pend, padded, cap, bm, tm):
    N, C, W = hn2.shape
    grid_spec = pltpu.PrefetchScalarGridSpec(
        num_scalar_prefetch=2,
        grid=(N // tm,),
        in_specs=[pl.BlockSpec((None, 1, 2 * tm), lambda i, pe, pa: (i, 0, 0), memory_space=pltpu.SMEM),
                  pl.BlockSpec((tm, C, W), lambda i, pe, pa: (i, 0, 0))],
        out_specs=pl.BlockSpec(memory_space=pl.ANY),
        scratch_shapes=[pltpu.VMEM((bm, C, W), F32), pltpu.SemaphoreType.DMA, pltpu.SemaphoreType.DMA],
    )
    return pl.pallas_call(
        functools.partial(_dispatch_body, tm=tm, bm=bm),
        grid_spec=grid_spec,
        out_shape=jax.ShapeDtypeStruct((cap, C, W), F32),
        compiler_params=_params("arbitrary"),
        name="dispatch",
    )(pend, padded, dest.reshape(N // tm, 1, 2 * tm), hn2)


def _expert_body(blk_e_ref, nact_ref, x_ref, wg_ref, wu_ref, wd_ref, y_ref):
    @pl.when(pl.program_id(0) < nact_ref[0])
    def _():
        x = _load_rows(x_ref).astype(BF16)
        gate = jnp.dot(x, wg_ref[...], preferred_element_type=F32)
        up = jnp.dot(x, wu_ref[...], preferred_element_type=F32)
        hidden = (jax.nn.silu(gate) * up).astype(BF16)
        _store_rows(y_ref, jnp.dot(hidden, wd_ref[...], preferred_element_type=F32))

    @pl.when(pl.program_id(0) >= nact_ref[0])
    def _():
        y_ref[...] = jnp.zeros_like(y_ref)


def _experts(buf, blk_e, nact, w_gate, w_up, w_down, bm):
    cap, C, W = buf.shape
    D, DE = w_gate.shape[1], w_gate.shape[2]
    rows = lambda i, be, na: (jnp.minimum(i, na[0] - 1), 0, 0)
    grid_spec = pltpu.PrefetchScalarGridSpec(
        num_scalar_prefetch=2,
        grid=(cap // bm,),
        in_specs=[pl.BlockSpec((bm, C, W), rows),
                  pl.BlockSpec((None, D, DE), lambda i, be, na: (be[i], 0, 0)),
                  pl.BlockSpec((None, D, DE), lambda i, be, na: (be[i], 0, 0)),
                  pl.BlockSpec((None, DE, D), lambda i, be, na: (be[i], 0, 0))],
        out_specs=pl.BlockSpec((bm, C, W), lambda i, be, na: (i, 0, 0)),
    )
    return pl.pallas_call(
        _expert_body,
        grid_spec=grid_spec,
        out_shape=jax.ShapeDtypeStruct((cap, C, W), F32),
        compiler_params=_params("arbitrary"),
        name="experts",
    )(blk_e, nact, buf, w_gate, w_up, w_down)


def _combine_body(dest_ref, h2_ref, rw_ref, yb_ref, y_ref, rows_ref, sem, *, tm):
    def issue(j, c):
        for u in range(ROW_DMA_UNROLL):
            r = j * ROW_DMA_UNROLL + u
            _row_copy(yb_ref, dest_ref[0, 2 * r], rows_ref.at[0], r, sem).start()
            _row_copy(yb_ref, dest_ref[0, 2 * r + 1], rows_ref.at[1], r, sem).start()
        return c

    lax.fori_loop(0, tm // ROW_DMA_UNROLL, issue, 0)

    def drain(j, c):
        for _ in range(2 * ROW_DMA_UNROLL):
            _row_copy(yb_ref, 0, rows_ref.at[0], 0, sem).wait()
        return c

    lax.fori_loop(0, tm // ROW_DMA_UNROLL, drain, 0)
    w = rw_ref[...]
    y_ref[...] = h2_ref[...] + (_load_rows(rows_ref.at[0]) * w[:, 0:1] + _load_rows(rows_ref.at[1]) * w[:, 1:2])


def _combine(h2, rw, dest, yb, tm):
    N, D = h2.shape
    C, W = yb.shape[1:]
    return pl.pallas_call(
        functools.partial(_combine_body, tm=tm),
        grid=(N // tm,),
        in_specs=[pl.BlockSpec((None, 1, 2 * tm), lambda i: (i, 0, 0), memory_space=pltpu.SMEM),
                  pl.BlockSpec((tm, D), lambda i: (i, 0)),
                  pl.BlockSpec((tm, ROUTER_LANES), lambda i: (i, 0)),
                  pl.BlockSpec(memory_space=pl.ANY)],
        out_specs=pl.BlockSpec((tm, D), lambda i: (i, 0)),
        out_shape=jax.ShapeDtypeStruct((N, D), F32),
        scratch_shapes=[pltpu.VMEM((2, tm, C, W), F32), pltpu.SemaphoreType.DMA],
        compiler_params=_params("arbitrary"),
        name="combine",
    )(dest.reshape(N // tm, 1, 2 * tm), h2, rw, yb)


def _moe(h2, hn2, ri, rw, counts, w_gate, w_up, w_down, tm, bm):
    N, D = h2.shape
    cnt = counts[0, N_GROUPS:N_GROUPS + N_EXPERTS].astype(jnp.int32)
    padded = (cnt + bm - 1) // bm * bm
    pend = jnp.cumsum(padded)
    pstart = pend - padded
    experts, rank = ri[:, 0:2], ri[:, 2:4]
    dest = (pstart[experts] + rank).astype(jnp.int32)
    cap = -(-(2 * N + N_EXPERTS * (bm - 1)) // bm) * bm
    nblk = cap // bm
    starts = jnp.arange(nblk, dtype=jnp.int32)[:, None] * bm
    blk_e = jnp.minimum(jnp.sum((pend[None, :] <= starts).astype(jnp.int32), axis=1), N_EXPERTS - 1)
    nact = (pend[-1:] // bm).astype(jnp.int32)
    buf = _dispatch(hn2, dest, pend.astype(jnp.int32), padded.astype(jnp.int32), cap, bm, tm)
    yb = _experts(buf, blk_e, nact, w_gate, w_up, w_down, bm)
    return _combine(h2, rw, dest, yb, tm)


def _constants(tm):
    r = lax.broadcasted_iota(jnp.int32, (KEY_BLOCK, KEY_BLOCK), 0)
    c = lax.broadcasted_iota(jnp.int32, (KEY_BLOCK, KEY_BLOCK), 1)
    tri = jnp.where(r > c, -1.0, 0.0).astype(BF16)
    hr = lax.broadcasted_iota(jnp.int32, (SB_WIDTH, SB_WIDTH), 0) // SB_HEAD_DIM
    hc = lax.broadcasted_iota(jnp.int32, (SB_WIDTH, SB_WIDTH), 1) // SB_HEAD_DIM
    hsum = jnp.where(hr == hc, 1.0 / SB_HEAD_DIM, 0.0).astype(BF16)
    lr = lax.broadcasted_iota(jnp.int32, (tm, tm), 0)
    lc = lax.broadcasted_iota(jnp.int32, (tm, tm), 1)
    lower = jnp.where(lc < lr, 1.0, 0.0).astype(BF16)
    return tri, hsum, lower


def kernel(x_prompt, x_sample, cache_sb_k, cache_sb_v, state_pool, meta_tokens, norm_mix, w_in, q_norm, k_norm,
           pool_mix, pool_scale, w_pool_out, w_sb_out, w_o, norm_ffn, w_router_group, b_router_group,
           w_router_expert, b_router_expert, w_gate, w_up, w_down):
    assert norm_mix.shape[0] == 1, "single-layer trunk"
    B, S, D = x_prompt.shape
    BS, SS, _ = x_sample.shape
    past = cache_sb_k.shape[2]
    tm_p, tm_s = min(512, B * S), min(512, BS * SS)
    tri, hsum, lower_p = _constants(tm_p)
    lower_s = lower_p if tm_s == tm_p else _constants(tm_s)[2]

    w_in0 = w_in[0]
    w_uqkv = w_in0[:, :4 * SB_WIDTH].astype(BF16)
    w_gates = w_in0[:, 4 * SB_WIDTH:].astype(BF16)
    nm = norm_mix[0][None, :]
    nf = norm_ffn[0][None, :]
    qg = jnp.tile(q_norm[0], SB_HEADS)[None, :]
    kg = jnp.tile(k_norm[0], SB_HEADS)[None, :]
    pm = pool_mix[0].astype(BF16)
    ps = pool_scale[0][None, :]
    wpo = w_pool_out[0].astype(BF16)
    wsb = w_sb_out[0].astype(BF16)
    wo = w_o[0].astype(BF16)
    pad = ROUTER_LANES - N_GROUPS - N_EXPERTS
    w_router = jnp.concatenate([w_router_group[0], w_router_expert[0], jnp.zeros((D, pad), F32)], axis=1).astype(BF16)
    b_router = jnp.concatenate([b_router_group[0], b_router_expert[0], jnp.zeros((pad,), F32)])[None, :]
    wg_e, wu_e, wd_e = w_gate[0].astype(BF16), w_up[0].astype(BF16), w_down[0].astype(BF16)

    inproj = functools.partial(_inproj, norm_mix=nm, w_uqkv=w_uqkv, qg=qg, kg=kg, hsum=hsum, pool_mix=pm,
                               pool_scale=ps)

    def tail(x, po, sb, lower, bm):
        n = x.shape[0] * x.shape[1]
        h2, hn2, ri, rw, counts = _merge_route(x.reshape(n, D), po.reshape(n, POOL_WIDTH), sb.reshape(n, SB_WIDTH),
                                               nm, w_gates, wpo, wsb, wo, nf, w_router, b_router, lower)
        return _moe(h2, hn2, ri, rw, counts, wg_e, wu_e, wd_e, lower.shape[0], bm).reshape(x.shape)

    shape5 = lambda a: a.reshape(1, a.shape[0], a.shape[1], SB_HEADS, SB_HEAD_DIM)

    prev_s = jnp.concatenate([jnp.zeros((BS, CARRY_ROWS - POOL_STATE, POOL_WIDTH), F32), state_pool[0]], axis=1)
    q_s, k_s, v_s, ks_f, vs_f, po_s, utail_s = inproj(x_sample, prev_s)
    kc = cache_sb_k[0].astype(BF16).reshape(BS, past, SB_WIDTH)
    vc = cache_sb_v[0].astype(BF16).reshape(BS, past, SB_WIDTH)
    sb_s = _attention(q_s, k_s, v_s, kc, vc, tri)
    y_sample = tail(x_sample, po_s, sb_s, lower_s, 128)
    new_pool_sample = utail_s[None, :, CARRY_ROWS - POOL_STATE:, :]

    zeros_prev = jnp.zeros((1, CARRY_ROWS, POOL_WIDTH), F32)
    _, km_b, vm_b, km_f, vm_f, _, u_meta = inproj(meta_tokens[None], zeros_prev)

    q_b, k_b, v_b, k_f, v_f, po, utail = inproj(x_prompt, u_meta)
    sb = _attention(q_b, k_b, v_b, km_b, vm_b, tri)
    y_prompt = tail(x_prompt, po, sb, lower_p, 512)
    new_k_prompt = shape5(jnp.concatenate([jnp.broadcast_to(km_f, (B, N_META, SB_WIDTH)), k_f], axis=1))
    new_v_prompt = shape5(jnp.concatenate([jnp.broadcast_to(vm_f, (B, N_META, SB_WIDTH)), v_f], axis=1))
    new_pool_prompt = utail[None, :, CARRY_ROWS - POOL_STATE:, :]

    return (y_prompt, y_sample, new_k_prompt, new_v_prompt, new_pool_prompt,
            shape5(ks_f), shape5(vs_f), new_pool_sample)
```

```python
import functools

import jax
import jax.numpy as jnp
from jax import lax
from jax.experimental import pallas as pl
from jax.experimental.pallas import tpu as pltpu

F32 = jnp.float32
BF16 = jnp.bfloat16

N_META = 16
POOL_WINDOWS = (2, 4, 8, 16)
POOL_GROUP = 128
POOL_WIDTH = 512
POOL_STATE = 15
SB_HEADS = 8
SB_HEAD_DIM = 64
SB_WIDTH = 512
HEAD_PAIR_LANES = 2 * SB_HEAD_DIM
N_GROUPS = 4
EXPERTS_PER_GROUP = 8
N_EXPERTS = 32
PAIRS_PER_GROUP = EXPERTS_PER_GROUP * (EXPERTS_PER_GROUP - 1) // 2
N_CLASSES = N_GROUPS * PAIRS_PER_GROUP
MOE_BLOCK_ROWS = 256
ROUTER_LANES = 128
ROW_TILE = (8, 128)
ROW_DMA_UNROLL = 8
NORM_EPS = 1e-6
CARRY_ROWS = 16
KEY_BLOCK = 256
VMEM_LIMIT_BYTES = 56 * 1024 * 1024


def _params(*sem):
    return pltpu.CompilerParams(dimension_semantics=sem, vmem_limit_bytes=VMEM_LIMIT_BYTES)


def _rms(x, gain):
    ms = jnp.mean(x * x, axis=-1, keepdims=True)
    return x * lax.rsqrt(ms + NORM_EPS) * gain


def _inproj_body(x_ref, nm_ref, w_ref, qg_ref, kg_ref, hs_ref, pm_ref, ps_ref, uprev_ref,
                 qb_ref, kb_ref, vb_ref, kf_ref, vf_ref, po_ref, ut_ref, carry_ref, *, tm):
    @pl.when(pl.program_id(1) == 0)
    def _():
        carry_ref[...] = uprev_ref[...]

    hn = _rms(x_ref[...], nm_ref[...]).astype(BF16)
    z = jnp.dot(hn, w_ref[...], preferred_element_type=F32)
    u = z[:, 0:POOL_WIDTH]
    v = z[:, 3 * SB_WIDTH:4 * SB_WIDTH]

    def head_norm(a, gain):
        ms = jnp.dot((a * a).astype(BF16), hs_ref[...], preferred_element_type=F32)
        return a * lax.rsqrt(ms + NORM_EPS) * gain

    qn = head_norm(z[:, SB_WIDTH:2 * SB_WIDTH], qg_ref[...])
    kn = head_norm(z[:, 2 * SB_WIDTH:3 * SB_WIDTH], kg_ref[...])
    qb_ref[...] = (qn * (SB_HEAD_DIM ** -0.5)).astype(BF16)
    kb_ref[...] = kn.astype(BF16)
    vb_ref[...] = v.astype(BF16)
    kf_ref[...] = kn
    vf_ref[...] = v

    ext = jnp.concatenate([carry_ref[...], u], axis=0)
    mixed = []
    for gi, w in enumerate(POOL_WINDOWS):
        c0, c1 = gi * POOL_GROUP, (gi + 1) * POOL_GROUP
        s = ext[:, c0:c1]
        shift = 1
        while shift < w:
            s = s + pltpu.roll(s, shift, axis=0)
            shift *= 2
        pooled = s[CARRY_ROWS:, :] * (1.0 / w) - u[:, c0:c1]
        mixed.append(jnp.dot(pooled.astype(BF16), pm_ref[gi], preferred_element_type=F32))
    po_ref[...] = (jnp.concatenate(mixed, axis=1) * ps_ref[...]).astype(BF16)
    carry_ref[...] = u[tm - CARRY_ROWS:, :]
    ut_ref[...] = u[tm - CARRY_ROWS:, :]


def _inproj(x, uprev, norm_mix, w_uqkv, qg, kg, hsum, pool_mix, pool_scale):
    B, S, D = x.shape
    tm = min(512, S)
    assert S % tm == 0 and tm % CARRY_ROWS == 0
    per_batch_prev = uprev.shape[0] == B and B > 1
    row = lambda b, t: (b, t, 0)
    const2 = lambda b, t: (0, 0)
    out_shape = (
        jax.ShapeDtypeStruct((B, S, SB_WIDTH), BF16),
        jax.ShapeDtypeStruct((B, S, SB_WIDTH), BF16),
        jax.ShapeDtypeStruct((B, S, SB_WIDTH), BF16),
        jax.ShapeDtypeStruct((B, S, SB_WIDTH), F32),
        jax.ShapeDtypeStruct((B, S, SB_WIDTH), F32),
        jax.ShapeDtypeStruct((B, S, POOL_WIDTH), BF16),
        jax.ShapeDtypeStruct((B, CARRY_ROWS, POOL_WIDTH), F32),
    )
    blk = lambda w: pl.BlockSpec((None, tm, w), row)
    return pl.pallas_call(
        functools.partial(_inproj_body, tm=tm),
        grid=(B, S // tm),
        in_specs=[
            pl.BlockSpec((None, tm, D), row),
            pl.BlockSpec((1, D), const2),
            pl.BlockSpec((D, 4 * SB_WIDTH), const2),
            pl.BlockSpec((1, SB_WIDTH), const2),
            pl.BlockSpec((1, SB_WIDTH), const2),
            pl.BlockSpec((SB_WIDTH, SB_WIDTH), const2),
            pl.BlockSpec((len(POOL_WINDOWS), POOL_GROUP, POOL_GROUP), lambda b, t: (0, 0, 0)),
            pl.BlockSpec((1, POOL_WIDTH), const2),
            pl.BlockSpec((None, CARRY_ROWS, POOL_WIDTH),
                         (lambda b, t: (b, 0, 0)) if per_batch_prev else (lambda b, t: (0, 0, 0))),
        ],
        out_specs=[blk(SB_WIDTH), blk(SB_WIDTH), blk(SB_WIDTH), blk(SB_WIDTH), blk(SB_WIDTH), blk(POOL_WIDTH),
                   pl.BlockSpec((None, CARRY_ROWS, POOL_WIDTH), lambda b, t: (b, 0, 0))],
        out_shape=out_shape,
        scratch_shapes=[pltpu.VMEM((CARRY_ROWS, POOL_WIDTH), F32)],
        compiler_params=_params("arbitrary", "arbitrary"),
        name="inproj",
    )(x, norm_mix, w_uqkv, qg, kg, hsum, pool_mix, pool_scale, uprev)


def _log_gates(z):
    sign = jnp.uint32(0x80000000)
    neg_abs = lax.bitcast_convert_type(lax.bitcast_convert_type(z, jnp.uint32) | sign, F32)
    sp = jnp.maximum(z, 0.0) + jnp.log(1.0 + jnp.exp(neg_abs))
    return sp, z - sp


def _dot_nt(a, b):
    return lax.dot_general(a, b, (((1,), (1,)), ((), ())), preferred_element_type=F32)


def _attn_body(vq_ref, vk_ref, q_ref, k_ref, v_ref, kp_ref, vp_ref, tri_ref, o_ref,
               z_scr, e_scr, acc_scr, run_scr, *, L, P, tq, n_visits, full_from_prefix):
    nq = L // tq
    lane = lax.broadcasted_iota(jnp.int32, (1, HEAD_PAIR_LANES), 1)
    heads = (lane < SB_HEAD_DIM, lane >= SB_HEAD_DIM)
    ksrc, vsrc = (kp_ref, vp_ref) if full_from_prefix else (k_ref, v_ref)

    def stacked_q(qi):
        qp = q_ref[pl.ds(pl.multiple_of(qi * tq, tq), tq), :]
        zero = jnp.zeros((), BF16)
        return jnp.concatenate([jnp.where(heads[0], qp, zero), jnp.where(heads[1], qp, zero)], axis=0)

    def key_rows(s):
        return pl.ds(pl.multiple_of(vk_ref[s] * KEY_BLOCK, KEY_BLOCK), KEY_BLOCK)

    def stage_a(s):
        z_scr[...] = _dot_nt(stacked_q(vq_ref[s]), ksrc[key_rows(s), :].astype(BF16))

    def stage_b(s):
        qi = vq_ref[s]
        sp, ls = _log_gates(z_scr[...])
        later = jnp.dot(sp.astype(BF16), tri_ref[...], preferred_element_type=F32)
        run = run_scr[qi]
        e_scr[...] = ls + (later + jnp.concatenate([run, run], axis=1))
        run_scr[qi] = run - jnp.sum(sp, axis=1, keepdims=True)

    def stage_c(s):
        qi = vq_ref[s]
        a = jnp.exp(e_scr[...]).astype(BF16)
        acc_scr[qi] = acc_scr[qi] + jnp.dot(a, vsrc[key_rows(s), :].astype(BF16), preferred_element_type=F32)

    if n_visits:
        acc_scr[...] = jnp.zeros_like(acc_scr)
        run_scr[...] = jnp.zeros_like(run_scr)
        stage_a(0)
        if n_visits >= 2:
            stage_b(0)
            stage_a(1)

        def steady(s, c):
            stage_c(s - 2)
            stage_b(s - 1)
            stage_a(s)
            return c

        lax.fori_loop(2, n_visits, steady, 0)
        if n_visits >= 2:
            stage_c(n_visits - 2)
        stage_b(n_visits - 1)
        stage_c(n_visits - 1)

    rows = lax.broadcasted_iota(jnp.int32, (2 * tq, tq), 0)
    cols = lax.broadcasted_iota(jnp.int32, (2 * tq, tq), 1)
    visible = (cols < rows) & (rows < tq) | (cols < rows - tq)

    def q_block(qi):
        qrows = pl.ds(pl.multiple_of(qi * tq, tq), tq)
        q2 = stacked_q(qi)
        sp, ls = _log_gates(_dot_nt(q2, k_ref[qrows, :]))
        sp = jnp.where(visible, sp, 0.0)
        later = jnp.dot(sp.astype(BF16), tri_ref[0:tq, 0:tq], preferred_element_type=F32)
        a = jnp.where(visible, jnp.exp(ls + later), 0.0)
        acc = jnp.dot(a.astype(BF16), v_ref[qrows, :], preferred_element_type=F32)
        run = -jnp.sum(sp, axis=1, keepdims=True)
        if n_visits:
            acc = acc + jnp.exp(run) * acc_scr[qi]
            run = run + run_scr[qi][:, 0:1]
        if not full_from_prefix:
            sp, ls = _log_gates(_dot_nt(q2, kp_ref[...].astype(BF16)))
            later = jnp.dot(sp.astype(BF16), tri_ref[0:P, 0:P], preferred_element_type=F32)
            a = jnp.exp(ls + (later + run))
            acc = acc + jnp.dot(a.astype(BF16), vp_ref[...].astype(BF16), preferred_element_type=F32)
        o_ref[qrows, :] = jnp.where(heads[0], acc[:tq], acc[tq:]).astype(BF16)

    def q_block_pair(j, carry):
        q_block(2 * j)
        q_block(2 * j + 1)
        return carry

    lax.fori_loop(0, nq // 2, q_block_pair, 0)
    if nq % 2:
        q_block(nq - 1)


def _attention(q, k, v, kp, vp, tri):
    B, L, _ = q.shape
    P = kp.shape[1]
    tq = min(KEY_BLOCK, L)
    nq = L // tq
    assert L % tq == 0 and tq % 16 == 0 and P % 16 == 0
    full_from_prefix = P > KEY_BLOCK
    if full_from_prefix:
        assert nq == 1 and P % KEY_BLOCK == 0
        visits = [(qi, kb) for qi in range(nq) for kb in reversed(range(P // KEY_BLOCK))]
    else:
        assert tq == KEY_BLOCK or nq == 1
        visits = [(qi, kb) for qi in range(nq) for kb in reversed(range(qi))]
    n_visits = len(visits)
    vq = jnp.asarray([qi for qi, _ in visits] or [0], jnp.int32)
    vk = jnp.asarray([kb for _, kb in visits] or [0], jnp.int32)
    per_batch_prefix = kp.shape[0] == B and B > 1
    seq = pl.BlockSpec((None, L, HEAD_PAIR_LANES), lambda b, hp, *_: (b, 0, hp))
    pre = pl.BlockSpec((None, P, HEAD_PAIR_LANES),
                       (lambda b, hp, *_: (b, 0, hp)) if per_batch_prefix else (lambda b, hp, *_: (0, 0, hp)))
    grid_spec = pltpu.PrefetchScalarGridSpec(
        num_scalar_prefetch=2,
        grid=(B, SB_WIDTH // HEAD_PAIR_LANES),
        in_specs=[seq, seq, seq, pre, pre, pl.BlockSpec((KEY_BLOCK, KEY_BLOCK), lambda b, hp, *_: (0, 0))],
        out_specs=seq,
        scratch_shapes=[pltpu.VMEM((2 * tq, KEY_BLOCK), F32), pltpu.VMEM((2 * tq, KEY_BLOCK), F32),
                        pltpu.VMEM((nq, 2 * tq, HEAD_PAIR_LANES), F32),
                        pltpu.VMEM((nq, 2 * tq, HEAD_PAIR_LANES), F32)],
    )
    return pl.pallas_call(
        functools.partial(_attn_body, L=L, P=P, tq=tq, n_visits=n_visits, full_from_prefix=full_from_prefix),
        grid_spec=grid_spec,
        out_shape=jax.ShapeDtypeStruct((B, L, SB_WIDTH), BF16),
        compiler_params=_params("arbitrary", "arbitrary"),
        name="stickbreak",
    )(vq, vk, q, k, v, kp, vp, tri)


def _transpose8(v):
    sub = lax.broadcasted_iota(jnp.int32, (1, ROW_TILE[0], 1), 1)
    for k in (4, 2, 1):
        low = (sub & k) == 0
        nxt = list(v)
        for i in range(ROW_TILE[0]):
            if i & k == 0:
                j = i | k
                nxt[i] = jnp.where(low, v[i], pltpu.roll(v[j], k, axis=1))
                nxt[j] = jnp.where(low, pltpu.roll(v[i], ROW_TILE[0] - k, axis=1), v[j])
        v = nxt
    return v


def _store_rows(ref, val):
    s, w = ROW_TILE
    g = val.shape[0] // s
    v = _transpose8([val[:, c * w:(c + 1) * w].reshape(g, s, w) for c in range(s)])
    for r in range(s):
        ref[pl.ds(r, g, stride=s), :, :] = v[r]


def _load_rows(ref):
    s, w = ROW_TILE
    g = ref.shape[0] // s
    v = _transpose8([ref[pl.ds(r, g, stride=s), :, :] for r in range(s)])
    return jnp.concatenate([x.reshape(g * s, w) for x in v], axis=1)


def _merge_body(xp_ref, pop_ref, sbp_ref, xs_ref, pos_ref, sbs_ref,
                nm_ref, wg_ref, wpo_ref, wsb_ref, wo_ref, nf_ref, wr_ref, br_ref, low_ref,
                h2_ref, ri_ref, rw_ref, cnt_ref, carry_ref, *, prompt_tiles):
    @pl.when(pl.program_id(0) == 0)
    def _():
        carry_ref[...] = jnp.zeros_like(carry_ref)

    from_prompt = pl.program_id(0) < prompt_tiles
    x = jnp.where(from_prompt, xp_ref[...], xs_ref[...])
    po = jnp.where(from_prompt, pop_ref[...], pos_ref[...])
    sb = jnp.where(from_prompt, sbp_ref[...], sbs_ref[...])
    D = x.shape[1]
    hn = _rms(x, nm_ref[...]).astype(BF16)
    gates = jnp.dot(hn, wg_ref[...], preferred_element_type=F32)
    pool_proj = jnp.dot(po, wpo_ref[...], preferred_element_type=F32)
    sb_proj = jnp.dot(sb, wsb_ref[...], preferred_element_type=F32)
    merged = jax.nn.sigmoid(gates[:, :D]) * pool_proj + jax.nn.sigmoid(gates[:, D:]) * sb_proj
    h2 = x + jnp.dot(merged.astype(BF16), wo_ref[...], preferred_element_type=F32)
    _store_rows(h2_ref, h2)
    hn2 = _rms(h2, nf_ref[...])

    logits = jnp.dot(hn2.astype(BF16), wr_ref[...], preferred_element_type=F32) + br_ref[...]
    lane = lax.broadcasted_iota(jnp.int32, logits.shape, 1)
    lanef = lane.astype(F32)
    neg = jnp.float32(-jnp.inf)
    big = jnp.float32(2 * ROUTER_LANES)

    def first_argmax(vals):
        m = jnp.max(vals, axis=1, keepdims=True)
        return m, jnp.min(jnp.where(vals == m, lanef, big), axis=1, keepdims=True)

    is_group = lane < N_GROUPS
    gmax, gidx = first_argmax(jnp.where(is_group, logits, neg))
    p_group = 1.0 / jnp.sum(jnp.where(is_group, jnp.exp(logits - gmax), 0.0), axis=1, keepdims=True)
    first = N_GROUPS + EXPERTS_PER_GROUP * gidx
    in_group = (lanef >= first) & (lanef < first + EXPERTS_PER_GROUP)
    cand = jnp.where(in_group, logits, neg)
    m1, i1 = first_argmax(cand)
    m2, i2 = first_argmax(jnp.where(lanef == i1, neg, cand))
    t = jnp.exp(m2 - m1)
    w1 = p_group / (1.0 + t)
    w2 = p_group * t / (1.0 + t)

    first_is_low = i1 < i2
    a = jnp.minimum(i1, i2) - first
    b = jnp.maximum(i1, i2) - first
    cls = gidx * PAIRS_PER_GROUP + a * (2 * EXPERTS_PER_GROUP - 1 - a) * 0.5 + (b - a - 1.0)
    hit = lanef == cls
    onehot = jnp.where(hit, 1.0, 0.0)
    before = jnp.dot(low_ref[...], onehot.astype(BF16), preferred_element_type=F32) + carry_ref[...]
    rank = jnp.sum(jnp.where(hit, before, 0.0), axis=1, keepdims=True)
    carry_ref[...] = carry_ref[...] + jnp.sum(onehot, axis=0, keepdims=True)
    cnt_ref[...] = carry_ref[...]

    ri_ref[...] = jnp.where(lane == 0, cls, jnp.where(lane == 1, rank, 0.0)).astype(jnp.int32)
    rw_ref[...] = jnp.where(lane == 0, jnp.where(first_is_low, w1, w2),
                            jnp.where(lane == 1, jnp.where(first_is_low, w2, w1), 0.0))


def _merge_route(prompt, sample, norm_mix, w_gates, w_pool_out, w_sb_out, w_o, norm_ffn, w_router, b_router, lower):
    tm = lower.shape[0]
    (xp, pop, sbp), (xs, pos, sbs) = prompt, sample
    D = xp.shape[1]
    assert xp.shape[0] % tm == 0 and xs.shape[0] % tm == 0 and D == ROW_TILE[0] * ROW_TILE[1]
    tp, ts = xp.shape[0] // tm, xs.shape[0] // tm
    N = xp.shape[0] + xs.shape[0]
    from_p = lambda i: (jnp.minimum(i, tp - 1), 0)
    from_s = lambda i: (jnp.maximum(i - tp, 0), 0)
    row = lambda i: (i, 0)
    const = lambda i: (0, 0)
    full = lambda a: pl.BlockSpec(a.shape, const)
    group = lambda m: [pl.BlockSpec((tm, D), m), pl.BlockSpec((tm, POOL_WIDTH), m), pl.BlockSpec((tm, SB_WIDTH), m)]
    return pl.pallas_call(
        functools.partial(_merge_body, prompt_tiles=tp),
        grid=(tp + ts,),
        in_specs=group(from_p) + group(from_s) + [
            full(norm_mix), full(w_gates), full(w_pool_out), full(w_sb_out), full(w_o), full(norm_ffn),
            full(w_router), full(b_router), full(lower)],
        out_specs=[pl.BlockSpec((tm,) + ROW_TILE, lambda i: (i, 0, 0)),
                   pl.BlockSpec((tm, ROUTER_LANES), row), pl.BlockSpec((tm, ROUTER_LANES), row),
                   pl.BlockSpec((1, ROUTER_LANES), const)],
        out_shape=(jax.ShapeDtypeStruct((N,) + ROW_TILE, F32),
                   jax.ShapeDtypeStruct((N, ROUTER_LANES), jnp.int32), jax.ShapeDtypeStruct((N, ROUTER_LANES), F32),
                   jax.ShapeDtypeStruct((1, ROUTER_LANES), F32)),
        scratch_shapes=[pltpu.VMEM((1, ROUTER_LANES), F32)],
        compiler_params=_params("arbitrary"),
        name="merge_route",
    )(xp, pop, sbp, xs, pos, sbs, norm_mix, w_gates, w_pool_out, w_sb_out, w_o, norm_ffn, w_router, b_router, lower)


def _row_copy(src_ref, src_row, dst_ref, dst_row, sem):
    return pltpu.make_async_copy(src_ref.at[src_row], dst_ref.at[dst_row], sem)


def _gather_rows(idx_ref, src_ref, dst_ref, sem, n):
    def issue(j, c):
        for u in range(ROW_DMA_UNROLL):
            r = j * ROW_DMA_UNROLL + u
            _row_copy(src_ref, idx_ref[0, r], dst_ref, r, sem).start()
        return c

    lax.fori_loop(0, n // ROW_DMA_UNROLL, issue, 0)


def _wait_rows(src_ref, dst_ref, sem, n):
    def drain(j, c):
        for _ in range(ROW_DMA_UNROLL):
            _row_copy(src_ref, 0, dst_ref, 0, sem).wait()
        return c

    lax.fori_loop(0, n // ROW_DMA_UNROLL, drain, 0)


def _pair_mlp_body(ea_ref, eb_ref, nact_ref, src_ref, nxt_ref, w_ref, h2_hbm, nf_ref,
                   wga_ref, wua_ref, wda_ref, wgb_ref, wub_ref, wdb_ref, y_ref, rows_ref, sems, *, bm):
    i = pl.program_id(0)
    nact = nact_ref[0]
    slot = i % 2

    @pl.when(i == 0)
    def _():
        _gather_rows(src_ref, h2_hbm, rows_ref.at[0], sems.at[0], bm)

    @pl.when(i + 1 < nact)
    def _():
        _gather_rows(nxt_ref, h2_hbm, rows_ref.at[1 - slot], sems.at[1 - slot], bm)

    @pl.when(i < nact)
    def _():
        _wait_rows(h2_hbm, rows_ref.at[slot], sems.at[slot], bm)
        h2 = _load_rows(rows_ref.at[slot])
        hn = _rms(h2, nf_ref[...]).astype(BF16)

        def mlp(wg_ref, wu_ref, wd_ref):
            gate = jnp.dot(hn, wg_ref[...], preferred_element_type=F32)
            up = jnp.dot(hn, wu_ref[...], preferred_element_type=F32)
            hidden = (jax.nn.silu(gate) * up).astype(BF16)
            return jnp.dot(hidden, wd_ref[...], preferred_element_type=F32)

        w = w_ref[...]
        out = mlp(wga_ref, wua_ref, wda_ref) * w[:, 0:1] + mlp(wgb_ref, wub_ref, wdb_ref) * w[:, 1:2]
        _store_rows(y_ref, h2 + out)

    @pl.when(i >= nact)
    def _():
        y_ref[...] = jnp.zeros_like(y_ref)


def _pair_mlp(h2rows, src, wsorted, blk_ea, blk_eb, nact, norm_ffn, w_gate, w_up, w_down, bm):
    cap = src.shape[0]
    nblk = cap // bm
    D, DE = w_gate.shape[1], w_gate.shape[2]
    act = lambda i, na: jnp.minimum(i, na[0] - 1)
    idx = lambda off: pl.BlockSpec((None, 1, bm), lambda i, ea, eb, na: (act(i + off, na), 0, 0),
                                   memory_space=pltpu.SMEM)
    wspec = lambda shape, which: pl.BlockSpec(
        (None,) + shape, (lambda i, ea, eb, na: (ea[i], 0, 0)) if which == 0 else (lambda i, ea, eb, na: (eb[i], 0, 0)))
    grid_spec = pltpu.PrefetchScalarGridSpec(
        num_scalar_prefetch=3,
        grid=(nblk,),
        in_specs=[idx(0), idx(1),
                  pl.BlockSpec((bm, 2), lambda i, ea, eb, na: (act(i, na), 0)),
                  pl.BlockSpec(memory_space=pl.ANY),
                  pl.BlockSpec(norm_ffn.shape, lambda i, ea, eb, na: (0, 0)),
                  wspec((D, DE), 0), wspec((D, DE), 0), wspec((DE, D), 0),
                  wspec((D, DE), 1), wspec((D, DE), 1), wspec((DE, D), 1)],
        out_specs=pl.BlockSpec((bm,) + ROW_TILE, lambda i, ea, eb, na: (i, 0, 0)),
        scratch_shapes=[pltpu.VMEM((2, bm) + ROW_TILE, F32), pltpu.SemaphoreType.DMA((2,))],
    )
    src3 = src.reshape(nblk, 1, bm)
    return pl.pallas_call(
        functools.partial(_pair_mlp_body, bm=bm),
        grid_spec=grid_spec,
        out_shape=jax.ShapeDtypeStruct((cap,) + ROW_TILE, F32),
        compiler_params=_params("arbitrary"),
        name="pair_mlp",
    )(blk_ea, blk_eb, nact, src3, src3, wsorted, h2rows, norm_ffn, w_gate, w_up, w_down, w_gate, w_up, w_down)


def _unsort_body(dest_ref, nxt_ref, ys_hbm, yp_ref, ysm_ref, rows_ref, sems, *, tm, prompt_tiles, n_tiles):
    i = pl.program_id(0)
    slot = i % 2

    @pl.when(i == 0)
    def _():
        _gather_rows(dest_ref, ys_hbm, rows_ref.at[0], sems.at[0], tm)

    @pl.when(i + 1 < n_tiles)
    def _():
        _gather_rows(nxt_ref, ys_hbm, rows_ref.at[1 - slot], sems.at[1 - slot], tm)

    _wait_rows(ys_hbm, rows_ref.at[slot], sems.at[slot], tm)
    y = _load_rows(rows_ref.at[slot])

    @pl.when(i < prompt_tiles)
    def _():
        yp_ref[...] = y

    @pl.when(i >= prompt_tiles)
    def _():
        ysm_ref[...] = y


def _unsort(ysorted, dest, tm, n_prompt):
    N = dest.shape[0]
    D = ROW_TILE[0] * ROW_TILE[1]
    n_tiles, tp = N // tm, n_prompt // tm
    dest3 = dest.reshape(n_tiles, 1, tm)
    idx = lambda off: pl.BlockSpec((None, 1, tm), lambda i: (jnp.minimum(i + off, n_tiles - 1), 0, 0),
                                   memory_space=pltpu.SMEM)
    return pl.pallas_call(
        functools.partial(_unsort_body, tm=tm, prompt_tiles=tp, n_tiles=n_tiles),
        grid=(n_tiles,),
        in_specs=[idx(0), idx(1), pl.BlockSpec(memory_space=pl.ANY)],
        out_specs=[pl.BlockSpec((tm, D), lambda i: (jnp.minimum(i, tp - 1), 0)),
                   pl.BlockSpec((tm, D), lambda i: (jnp.maximum(i - tp, 0), 0))],
        out_shape=(jax.ShapeDtypeStruct((n_prompt, D), F32), jax.ShapeDtypeStruct((N - n_prompt, D), F32)),
        scratch_shapes=[pltpu.VMEM((2, tm) + ROW_TILE, F32), pltpu.SemaphoreType.DMA((2,))],
        compiler_params=_params("arbitrary"),
        name="unsort",
    )(dest3, dest3, ysorted)


def _pair_tables():
    ea, eb = [], []
    for g in range(N_GROUPS):
        for a in range(EXPERTS_PER_GROUP):
            for b in range(a + 1, EXPERTS_PER_GROUP):
                ea.append(g * EXPERTS_PER_GROUP + a)
                eb.append(g * EXPERTS_PER_GROUP + b)
    return jnp.asarray(ea, jnp.int32), jnp.asarray(eb, jnp.int32)


def _moe(h2rows, ri, rw, counts, norm_ffn, w_gate, w_up, w_down, tm, n_prompt, bm):
    N = h2rows.shape[0]
    cnt = counts[0, :N_CLASSES].astype(jnp.int32)
    padded = (cnt + bm - 1) // bm * bm
    cend = jnp.cumsum(padded)
    cstart = cend - padded
    dest = (cstart[ri[:, 0]] + ri[:, 1]).astype(jnp.int32)
    cap = -(-(N + N_CLASSES * (bm - 1)) // bm) * bm
    nblk = cap // bm
    starts = jnp.arange(nblk, dtype=jnp.int32)[:, None] * bm
    blk_cls = jnp.minimum(jnp.sum((cend[None, :] <= starts).astype(jnp.int32), axis=1), N_CLASSES - 1)
    ea_of, eb_of = _pair_tables()
    nact = (cend[-1:] // bm).astype(jnp.int32)
    src = jnp.zeros((cap,), jnp.int32).at[dest].set(jnp.arange(N, dtype=jnp.int32))
    ysorted = _pair_mlp(h2rows, src, rw[:, 0:2][src], ea_of[blk_cls], eb_of[blk_cls], nact, norm_ffn,
                        w_gate, w_up, w_down, bm)
    return _unsort(ysorted, dest, tm, n_prompt)


def _constants(tm):
    r = lax.broadcasted_iota(jnp.int32, (KEY_BLOCK, KEY_BLOCK), 0)
    c = lax.broadcasted_iota(jnp.int32, (KEY_BLOCK, KEY_BLOCK), 1)
    tri = jnp.where(r > c, -1.0, 0.0).astype(BF16)
    hr = lax.broadcasted_iota(jnp.int32, (SB_WIDTH, SB_WIDTH), 0) // SB_HEAD_DIM
    hc = lax.broadcasted_iota(jnp.int32, (SB_WIDTH, SB_WIDTH), 1) // SB_HEAD_DIM
    hsum = jnp.where(hr == hc, 1.0 / SB_HEAD_DIM, 0.0).astype(BF16)
    lr = lax.broadcasted_iota(jnp.int32, (tm, tm), 0)
    lc = lax.broadcasted_iota(jnp.int32, (tm, tm), 1)
    lower = jnp.where(lc < lr, 1.0, 0.0).astype(BF16)
    return tri, hsum, lower


def kernel(x_prompt, x_sample, cache_sb_k, cache_sb_v, state_pool, meta_tokens, norm_mix, w_in, q_norm, k_norm,
           pool_mix, pool_scale, w_pool_out, w_sb_out, w_o, norm_ffn, w_router_group, b_router_group,
           w_router_expert, b_router_expert, w_gate, w_up, w_down):
    assert norm_mix.shape[0] == 1, "single-layer trunk"
    B, S, D = x_prompt.shape
    BS, SS, _ = x_sample.shape
    past = cache_sb_k.shape[2]
    tm = min(512, B * S, BS * SS)
    tri, hsum, lower = _constants(tm)

    w_in0 = w_in[0]
    w_uqkv = w_in0[:, :4 * SB_WIDTH].astype(BF16)
    w_gates = w_in0[:, 4 * SB_WIDTH:].astype(BF16)
    nm = norm_mix[0][None, :]
    nf = norm_ffn[0][None, :]
    qg = jnp.tile(q_norm[0], SB_HEADS)[None, :]
    kg = jnp.tile(k_norm[0], SB_HEADS)[None, :]
    pm = pool_mix[0].astype(BF16)
    ps = pool_scale[0][None, :]
    wpo = w_pool_out[0].astype(BF16)
    wsb = w_sb_out[0].astype(BF16)
    wo = w_o[0].astype(BF16)
    pad = ROUTER_LANES - N_GROUPS - N_EXPERTS
    w_router = jnp.concatenate([w_router_group[0], w_router_expert[0], jnp.zeros((D, pad), F32)], axis=1).astype(BF16)
    b_router = jnp.concatenate([b_router_group[0], b_router_expert[0], jnp.zeros((pad,), F32)])[None, :]
    wg_e, wu_e, wd_e = w_gate[0].astype(BF16), w_up[0].astype(BF16), w_down[0].astype(BF16)

    inproj = functools.partial(_inproj, norm_mix=nm, w_uqkv=w_uqkv, qg=qg, kg=kg, hsum=hsum, pool_mix=pm,
                               pool_scale=ps)

    shape5 = lambda a: a.reshape(1, a.shape[0], a.shape[1], SB_HEADS, SB_HEAD_DIM)
    flat = lambda a: a.reshape(a.shape[0] * a.shape[1], a.shape[2])

    prev_s = jnp.concatenate([jnp.zeros((BS, CARRY_ROWS - POOL_STATE, POOL_WIDTH), F32), state_pool[0]], axis=1)
    q_s, k_s, v_s, ks_f, vs_f, po_s, utail_s = inproj(x_sample, prev_s)
    kc = cache_sb_k[0].astype(BF16).reshape(BS, past, SB_WIDTH)
    vc = cache_sb_v[0].astype(BF16).reshape(BS, past, SB_WIDTH)
    sb_s = _attention(q_s, k_s, v_s, kc, vc, tri)
    new_pool_sample = utail_s[None, :, CARRY_ROWS - POOL_STATE:, :]

    zeros_prev = jnp.zeros((1, CARRY_ROWS, POOL_WIDTH), F32)
    _, km_b, vm_b, km_f, vm_f, _, u_meta = inproj(meta_tokens[None], zeros_prev)

    q_b, k_b, v_b, k_f, v_f, po, utail = inproj(x_prompt, u_meta)
    sb = _attention(q_b, k_b, v_b, km_b, vm_b, tri)
    new_k_prompt = shape5(jnp.concatenate([jnp.broadcast_to(km_f, (B, N_META, SB_WIDTH)), k_f], axis=1))
    new_v_prompt = shape5(jnp.concatenate([jnp.broadcast_to(vm_f, (B, N_META, SB_WIDTH)), v_f], axis=1))
    new_pool_prompt = utail[None, :, CARRY_ROWS - POOL_STATE:, :]

    h2rows, ri, rw, counts = _merge_route((flat(x_prompt), flat(po), flat(sb)), (flat(x_sample), flat(po_s), flat(sb_s)),
                                          nm, w_gates, wpo, wsb, wo, nf, w_router, b_router, lower)
    y_p, y_s = _moe(h2rows, ri, rw, counts, nf, wg_e, wu_e, wd_e, tm, B * S, MOE_BLOCK_ROWS)

    return (y_p.reshape(x_prompt.shape), y_s.reshape(x_sample.shape), new_k_prompt, new_v_prompt, new_pool_prompt,
            shape5(ks_f), shape5(vs_f), new_pool_sample)
```

```python
import functools

import jax
import jax.numpy as jnp
from jax import lax
from jax.experimental import pallas as pl
from jax.experimental.pallas import tpu as pltpu

F32 = jnp.float32
BF16 = jnp.bfloat16

N_META = 16
POOL_WINDOWS = (2, 4, 8, 16)
POOL_GROUP = 128
POOL_WIDTH = 512
POOL_STATE = 15
SB_HEADS = 8
SB_HEAD_DIM = 64
SB_WIDTH = 512
HEAD_PAIR_LANES = 2 * SB_HEAD_DIM
N_GROUPS = 4
EXPERTS_PER_GROUP = 8
N_EXPERTS = 32
ROUTER_LANES = 128
MOE_BLOCK_ROWS = 512
ROW_TILE = (8, 128)
ROW_DMA_UNROLL = 8
NORM_EPS = 1e-6
CARRY_ROWS = 16
KEY_BLOCK = 256
VMEM_LIMIT_BYTES = 56 * 1024 * 1024


def _params(*sem):
    return pltpu.CompilerParams(dimension_semantics=sem, vmem_limit_bytes=VMEM_LIMIT_BYTES)


def _rms(x, gain):
    ms = jnp.mean(x * x, axis=-1, keepdims=True)
    return x * lax.rsqrt(ms + NORM_EPS) * gain


def _inproj_body(x_ref, nm_ref, w_ref, qg_ref, kg_ref, hs_ref, pm_ref, ps_ref, uprev_ref, klead_ref, vlead_ref,
                 qb_ref, kb_ref, vb_ref, kf_ref, vf_ref, po_ref, ut_ref, carry_ref, kcarry_ref, vcarry_ref,
                 *, tm, n_tiles, lead):
    t = pl.program_id(1)

    @pl.when(t == 0)
    def _():
        carry_ref[...] = uprev_ref[...]
        if lead:
            kcarry_ref[...] = klead_ref[...]
            vcarry_ref[...] = vlead_ref[...]

    if lead:
        @pl.when(t == n_tiles)
        def _():
            kf_ref[0:lead, :] = kcarry_ref[...]
            vf_ref[0:lead, :] = vcarry_ref[...]

    @pl.when(t < n_tiles)
    def _():
        _inproj_tile(x_ref, nm_ref, w_ref, qg_ref, kg_ref, hs_ref, pm_ref, ps_ref,
                     qb_ref, kb_ref, vb_ref, kf_ref, vf_ref, po_ref, ut_ref, carry_ref, kcarry_ref, vcarry_ref,
                     tm=tm, lead=lead)


def _inproj_tile(x_ref, nm_ref, w_ref, qg_ref, kg_ref, hs_ref, pm_ref, ps_ref,
                 qb_ref, kb_ref, vb_ref, kf_ref, vf_ref, po_ref, ut_ref, carry_ref, kcarry_ref, vcarry_ref, *, tm, lead):
    hn = _rms(x_ref[...], nm_ref[...]).astype(BF16)
    z = jnp.dot(hn, w_ref[...], preferred_element_type=F32)
    u = z[:, 0:POOL_WIDTH]
    v = z[:, 3 * SB_WIDTH:4 * SB_WIDTH]

    def head_norm(a, gain):
        ms = jnp.dot((a * a).astype(BF16), hs_ref[...], preferred_element_type=F32)
        return a * lax.rsqrt(ms + NORM_EPS) * gain

    qn = head_norm(z[:, SB_WIDTH:2 * SB_WIDTH], qg_ref[...])
    kn = head_norm(z[:, 2 * SB_WIDTH:3 * SB_WIDTH], kg_ref[...])
    qb_ref[...] = (qn * (SB_HEAD_DIM ** -0.5)).astype(BF16)
    kb_ref[...] = kn.astype(BF16)
    vb_ref[...] = v.astype(BF16)
    if lead:
        kf_ref[...] = jnp.concatenate([kcarry_ref[...], kn[:tm - lead]], axis=0)
        vf_ref[...] = jnp.concatenate([vcarry_ref[...], v[:tm - lead]], axis=0)
        kcarry_ref[...] = kn[tm - lead:]
        vcarry_ref[...] = v[tm - lead:]
    else:
        kf_ref[...] = kn
        vf_ref[...] = v

    ext = jnp.concatenate([carry_ref[...], u], axis=0)
    mixed = []
    for gi, w in enumerate(POOL_WINDOWS):
        c0, c1 = gi * POOL_GROUP, (gi + 1) * POOL_GROUP
        s = ext[:, c0:c1]
        shift = 1
        while shift < w:
            s = s + pltpu.roll(s, shift, axis=0)
            shift *= 2
        pooled = s[CARRY_ROWS:, :] * (1.0 / w) - u[:, c0:c1]
        mixed.append(jnp.dot(pooled.astype(BF16), pm_ref[gi], preferred_element_type=F32))
    po_ref[...] = (jnp.concatenate(mixed, axis=1) * ps_ref[...]).astype(BF16)
    carry_ref[...] = u[tm - CARRY_ROWS:, :]
    ut_ref[...] = u[tm - CARRY_ROWS:, :]


def _inproj(x, uprev, norm_mix, w_uqkv, qg, kg, hsum, pool_mix, pool_scale, k_lead=None, v_lead=None):
    B, S, D = x.shape
    tm = min(512, S)
    n_tiles = S // tm
    lead = 0 if k_lead is None else k_lead.shape[1]
    assert S % tm == 0 and tm % CARRY_ROWS == 0 and lead % ROW_TILE[0] == 0 and lead < tm
    if k_lead is None:
        k_lead = v_lead = jnp.zeros((1, ROW_TILE[0], SB_WIDTH), F32)
    per_batch_prev = uprev.shape[0] == B and B > 1
    last = n_tiles - 1
    row = lambda b, t: (b, jnp.minimum(t, last), 0)
    late = lambda b, t: (b, t, 0)
    const2 = lambda b, t: (0, 0)
    out_shape = (
        jax.ShapeDtypeStruct((B, S, SB_WIDTH), BF16),
        jax.ShapeDtypeStruct((B, S, SB_WIDTH), BF16),
        jax.ShapeDtypeStruct((B, S, SB_WIDTH), BF16),
        jax.ShapeDtypeStruct((B, S + lead, SB_WIDTH), F32),
        jax.ShapeDtypeStruct((B, S + lead, SB_WIDTH), F32),
        jax.ShapeDtypeStruct((B, S, POOL_WIDTH), BF16),
        jax.ShapeDtypeStruct((B, CARRY_ROWS, POOL_WIDTH), F32),
    )
    blk = lambda w: pl.BlockSpec((None, tm, w), row)
    return pl.pallas_call(
        functools.partial(_inproj_body, tm=tm, n_tiles=n_tiles, lead=lead),
        grid=(B, n_tiles + (1 if lead else 0)),
        in_specs=[
            pl.BlockSpec((None, tm, D), row),
            pl.BlockSpec((1, D), const2),
            pl.BlockSpec((D, 4 * SB_WIDTH), const2),
            pl.BlockSpec((1, SB_WIDTH), const2),
            pl.BlockSpec((1, SB_WIDTH), const2),
            pl.BlockSpec((SB_WIDTH, SB_WIDTH), const2),
            pl.BlockSpec((len(POOL_WINDOWS), POOL_GROUP, POOL_GROUP), lambda b, t: (0, 0, 0)),
            pl.BlockSpec((1, POOL_WIDTH), const2),
            pl.BlockSpec((None, CARRY_ROWS, POOL_WIDTH),
                         (lambda b, t: (b, 0, 0)) if per_batch_prev else (lambda b, t: (0, 0, 0))),
            pl.BlockSpec((None,) + k_lead.shape[1:], lambda b, t: (0, 0, 0)),
            pl.BlockSpec((None,) + v_lead.shape[1:], lambda b, t: (0, 0, 0)),
        ],
        out_specs=[blk(SB_WIDTH), blk(SB_WIDTH), blk(SB_WIDTH),
                   pl.BlockSpec((None, tm, SB_WIDTH), late), pl.BlockSpec((None, tm, SB_WIDTH), late), blk(POOL_WIDTH),
                   pl.BlockSpec((None, CARRY_ROWS, POOL_WIDTH), lambda b, t: (b, 0, 0))],
        out_shape=out_shape,
        scratch_shapes=[pltpu.VMEM((CARRY_ROWS, POOL_WIDTH), F32),
                        pltpu.VMEM(k_lead.shape[1:], F32), pltpu.VMEM(v_lead.shape[1:], F32)],
        compiler_params=_params("arbitrary", "arbitrary"),
        name="inproj",
    )(x, norm_mix, w_uqkv, qg, kg, hsum, pool_mix, pool_scale, uprev, k_lead, v_lead)


def _log_gates(z):
    sign = jnp.uint32(0x80000000)
    neg_abs = lax.bitcast_convert_type(lax.bitcast_convert_type(z, jnp.uint32) | sign, F32)
    sp = jnp.maximum(z, 0.0) + jnp.log(1.0 + jnp.exp(neg_abs))
    return sp, z - sp


def _dot_nt(a, b):
    return lax.dot_general(a, b, (((1,), (1,)), ((), ())), preferred_element_type=F32)


def _attn_body(vq_ref, vk_ref, q_ref, k_ref, v_ref, kp_ref, vp_ref, tri_ref, o_ref,
               z_scr, e_scr, acc_scr, run_scr, *, L, P, tq, n_visits, full_from_prefix):
    nq = L // tq
    lane = lax.broadcasted_iota(jnp.int32, (1, HEAD_PAIR_LANES), 1)
    heads = (lane < SB_HEAD_DIM, lane >= SB_HEAD_DIM)
    ksrc, vsrc = (kp_ref, vp_ref) if full_from_prefix else (k_ref, v_ref)

    def stacked_q(qi):
        qp = q_ref[pl.ds(pl.multiple_of(qi * tq, tq), tq), :]
        zero = jnp.zeros((), BF16)
        return jnp.concatenate([jnp.where(heads[0], qp, zero), jnp.where(heads[1], qp, zero)], axis=0)

    def key_rows(s):
        return pl.ds(pl.multiple_of(vk_ref[s] * KEY_BLOCK, KEY_BLOCK), KEY_BLOCK)

    def stage_a(s):
        z_scr[...] = _dot_nt(stacked_q(vq_ref[s]), ksrc[key_rows(s), :].astype(BF16))

    def stage_b(s):
        qi = vq_ref[s]
        sp, ls = _log_gates(z_scr[...])
        later = jnp.dot(sp.astype(BF16), tri_ref[...], preferred_element_type=F32)
        run = run_scr[qi]
        e_scr[...] = ls + (later + jnp.concatenate([run, run], axis=1))
        run_scr[qi] = run - jnp.sum(sp, axis=1, keepdims=True)

    def stage_c(s):
        qi = vq_ref[s]
        a = jnp.exp(e_scr[...]).astype(BF16)
        acc_scr[qi] = acc_scr[qi] + jnp.dot(a, vsrc[key_rows(s), :].astype(BF16), preferred_element_type=F32)

    if n_visits:
        acc_scr[...] = jnp.zeros_like(acc_scr)
        run_scr[...] = jnp.zeros_like(run_scr)
        stage_a(0)
        if n_visits >= 2:
            stage_b(0)
            stage_a(1)

        def steady(s, c):
            stage_c(s - 2)
            stage_b(s - 1)
            stage_a(s)
            return c

        lax.fori_loop(2, n_visits, steady, 0)
        if n_visits >= 2:
            stage_c(n_visits - 2)
        stage_b(n_visits - 1)
        stage_c(n_visits - 1)

    rows = lax.broadcasted_iota(jnp.int32, (2 * tq, tq), 0)
    cols = lax.broadcasted_iota(jnp.int32, (2 * tq, tq), 1)
    visible = (cols < rows) & (rows < tq) | (cols < rows - tq)

    def q_block(qi):
        qrows = pl.ds(pl.multiple_of(qi * tq, tq), tq)
        q2 = stacked_q(qi)
        sp, ls = _log_gates(_dot_nt(q2, k_ref[qrows, :]))
        sp = jnp.where(visible, sp, 0.0)
        later = jnp.dot(sp.astype(BF16), tri_ref[0:tq, 0:tq], preferred_element_type=F32)
        a = jnp.where(visible, jnp.exp(ls + later), 0.0)
        acc = jnp.dot(a.astype(BF16), v_ref[qrows, :], preferred_element_type=F32)
        run = -jnp.sum(sp, axis=1, keepdims=True)
        if n_visits:
            acc = acc + jnp.exp(run) * acc_scr[qi]
            run = run + run_scr[qi][:, 0:1]
        if not full_from_prefix:
            sp, ls = _log_gates(_dot_nt(q2, kp_ref[...].astype(BF16)))
            later = jnp.dot(sp.astype(BF16), tri_ref[0:P, 0:P], preferred_element_type=F32)
            a = jnp.exp(ls + (later + run))
            acc = acc + jnp.dot(a.astype(BF16), vp_ref[...].astype(BF16), preferred_element_type=F32)
        o_ref[qrows, :] = jnp.where(heads[0], acc[:tq], acc[tq:]).astype(BF16)

    def q_block_pair(j, carry):
        q_block(2 * j)
        q_block(2 * j + 1)
        return carry

    lax.fori_loop(0, nq // 2, q_block_pair, 0)
    if nq % 2:
        q_block(nq - 1)


def _attention(q, k, v, kp, vp, tri):
    B, L, _ = q.shape
    P = kp.shape[1]
    tq = min(KEY_BLOCK, L)
    nq = L // tq
    assert L % tq == 0 and tq % 16 == 0 and P % 16 == 0
    full_from_prefix = P > KEY_BLOCK
    if full_from_prefix:
        assert nq == 1 and P % KEY_BLOCK == 0
        visits = [(qi, kb) for qi in range(nq) for kb in reversed(range(P // KEY_BLOCK))]
    else:
        assert tq == KEY_BLOCK or nq == 1
        visits = [(qi, kb) for qi in range(nq) for kb in reversed(range(qi))]
    n_visits = len(visits)
    vq = jnp.asarray([qi for qi, _ in visits] or [0], jnp.int32)
    vk = jnp.asarray([kb for _, kb in visits] or [0], jnp.int32)
    per_batch_prefix = kp.shape[0] == B and B > 1
    seq = pl.BlockSpec((None, L, HEAD_PAIR_LANES), lambda b, hp, *_: (b, 0, hp))
    pre = pl.BlockSpec((None, P, HEAD_PAIR_LANES),
                       (lambda b, hp, *_: (b, 0, hp)) if per_batch_prefix else (lambda b, hp, *_: (0, 0, hp)))
    grid_spec = pltpu.PrefetchScalarGridSpec(
        num_scalar_prefetch=2,
        grid=(B, SB_WIDTH // HEAD_PAIR_LANES),
        in_specs=[seq, seq, seq, pre, pre, pl.BlockSpec((KEY_BLOCK, KEY_BLOCK), lambda b, hp, *_: (0, 0))],
        out_specs=seq,
        scratch_shapes=[pltpu.VMEM((2 * tq, KEY_BLOCK), F32), pltpu.VMEM((2 * tq, KEY_BLOCK), F32),
                        pltpu.VMEM((nq, 2 * tq, HEAD_PAIR_LANES), F32),
                        pltpu.VMEM((nq, 2 * tq, HEAD_PAIR_LANES), F32)],
    )
    return pl.pallas_call(
        functools.partial(_attn_body, L=L, P=P, tq=tq, n_visits=n_visits, full_from_prefix=full_from_prefix),
        grid_spec=grid_spec,
        out_shape=jax.ShapeDtypeStruct((B, L, SB_WIDTH), BF16),
        compiler_params=_params("arbitrary", "arbitrary"),
        name="stickbreak",
    )(vq, vk, q, k, v, kp, vp, tri)


def _transpose8(v):
    sub = lax.broadcasted_iota(jnp.int32, (1, ROW_TILE[0], 1), 1)
    for k in (4, 2, 1):
        low = (sub & k) == 0
        nxt = list(v)
        for i in range(ROW_TILE[0]):
            if i & k == 0:
                j = i | k
                nxt[i] = jnp.where(low, v[i], pltpu.roll(v[j], k, axis=1))
                nxt[j] = jnp.where(low, pltpu.roll(v[i], ROW_TILE[0] - k, axis=1), v[j])
        v = nxt
    return v


def _store_rows(ref, val):
    s, w = ROW_TILE
    g = val.shape[0] // s
    v = _transpose8([val[:, c * w:(c + 1) * w].reshape(g, s, w) for c in range(s)])
    for r in range(s):
        ref[pl.ds(r, g, stride=s), :, :] = v[r]


def _load_rows(ref):
    s, w = ROW_TILE
    g = ref.shape[0] // s
    v = _transpose8([ref[pl.ds(r, g, stride=s), :, :] for r in range(s)])
    return jnp.concatenate([x.reshape(g * s, w) for x in v], axis=1)


def _merge_body(xp_ref, pop_ref, sbp_ref, xs_ref, pos_ref, sbs_ref,
                nm_ref, wg_ref, wpo_ref, wsb_ref, wo_ref, nf_ref, wr_ref, br_ref, low_ref,
                h2_ref, hn2_ref, ri_ref, rw_ref, cnt_ref, carry_ref, *, prompt_tiles):
    @pl.when(pl.program_id(0) == 0)
    def _():
        carry_ref[...] = jnp.zeros_like(carry_ref)

    from_prompt = pl.program_id(0) < prompt_tiles
    x = jnp.where(from_prompt, xp_ref[...], xs_ref[...])
    po = jnp.where(from_prompt, pop_ref[...], pos_ref[...])
    sb = jnp.where(from_prompt, sbp_ref[...], sbs_ref[...])
    D = x.shape[1]
    hn = _rms(x, nm_ref[...]).astype(BF16)
    gates = jnp.dot(hn, wg_ref[...], preferred_element_type=F32)
    pool_proj = jnp.dot(po, wpo_ref[...], preferred_element_type=F32)
    sb_proj = jnp.dot(sb, wsb_ref[...], preferred_element_type=F32)
    merged = jax.nn.sigmoid(gates[:, :D]) * pool_proj + jax.nn.sigmoid(gates[:, D:]) * sb_proj
    h2 = x + jnp.dot(merged.astype(BF16), wo_ref[...], preferred_element_type=F32)
    h2_ref[...] = h2
    hn2 = _rms(h2, nf_ref[...])
    _store_rows(hn2_ref, hn2)

    logits = jnp.dot(hn2.astype(BF16), wr_ref[...], preferred_element_type=F32) + br_ref[...]
    lane = lax.broadcasted_iota(jnp.int32, logits.shape, 1)
    lanef = lane.astype(F32)
    neg = jnp.float32(-jnp.inf)
    big = jnp.float32(2 * ROUTER_LANES)

    def first_argmax(vals):
        m = jnp.max(vals, axis=1, keepdims=True)
        return m, jnp.min(jnp.where(vals == m, lanef, big), axis=1, keepdims=True)

    is_group = lane < N_GROUPS
    gmax, gidx = first_argmax(jnp.where(is_group, logits, neg))
    p_group = 1.0 / jnp.sum(jnp.where(is_group, jnp.exp(logits - gmax), 0.0), axis=1, keepdims=True)
    first = N_GROUPS + EXPERTS_PER_GROUP * gidx
    in_group = (lanef >= first) & (lanef < first + EXPERTS_PER_GROUP)
    cand = jnp.where(in_group, logits, neg)
    m1, i1 = first_argmax(cand)
    m2, i2 = first_argmax(jnp.where(lanef == i1, neg, cand))
    t = jnp.exp(m2 - m1)
    w1 = p_group / (1.0 + t)
    w2 = p_group * t / (1.0 + t)

    hit1 = lanef == i1
    hit2 = lanef == i2
    onehot = jnp.where(hit1 | hit2, 1.0, 0.0)
    before = jnp.dot(low_ref[...], onehot.astype(BF16), preferred_element_type=F32) + carry_ref[...]
    r1 = jnp.sum(jnp.where(hit1, before, 0.0), axis=1, keepdims=True)
    r2 = jnp.sum(jnp.where(hit2, before, 0.0), axis=1, keepdims=True)
    carry_ref[...] = carry_ref[...] + jnp.sum(onehot, axis=0, keepdims=True)
    cnt_ref[...] = carry_ref[...]

    slab = jnp.where(lane == 0, i1 - N_GROUPS,
                     jnp.where(lane == 1, i2 - N_GROUPS, jnp.where(lane == 2, r1, jnp.where(lane == 3, r2, 0.0))))
    ri_ref[...] = slab.astype(jnp.int32)
    rw_ref[...] = jnp.where(lane == 0, w1, jnp.where(lane == 1, w2, 0.0))


def _merge_route(prompt, sample, norm_mix, w_gates, w_pool_out, w_sb_out, w_o, norm_ffn, w_router, b_router, lower):
    tm = lower.shape[0]
    (xp, pop, sbp), (xs, pos, sbs) = prompt, sample
    D = xp.shape[1]
    assert xp.shape[0] % tm == 0 and xs.shape[0] % tm == 0 and D == ROW_TILE[0] * ROW_TILE[1]
    tp, ts = xp.shape[0] // tm, xs.shape[0] // tm
    N = xp.shape[0] + xs.shape[0]
    from_p = lambda i: (jnp.minimum(i, tp - 1), 0)
    from_s = lambda i: (jnp.maximum(i - tp, 0), 0)
    row = lambda i: (i, 0)
    const = lambda i: (0, 0)
    full = lambda a: pl.BlockSpec(a.shape, const)
    group = lambda m: [pl.BlockSpec((tm, D), m), pl.BlockSpec((tm, POOL_WIDTH), m), pl.BlockSpec((tm, SB_WIDTH), m)]
    return pl.pallas_call(
        functools.partial(_merge_body, prompt_tiles=tp),
        grid=(tp + ts,),
        in_specs=group(from_p) + group(from_s) + [
            full(norm_mix), full(w_gates), full(w_pool_out), full(w_sb_out), full(w_o), full(norm_ffn),
            full(w_router), full(b_router), full(lower)],
        out_specs=[pl.BlockSpec((tm, D), row), pl.BlockSpec((tm,) + ROW_TILE, lambda i: (i, 0, 0)),
                   pl.BlockSpec((tm, ROUTER_LANES), row), pl.BlockSpec((tm, ROUTER_LANES), row),
                   pl.BlockSpec((1, ROUTER_LANES), const)],
        out_shape=(jax.ShapeDtypeStruct((N, D), F32), jax.ShapeDtypeStruct((N,) + ROW_TILE, F32),
                   jax.ShapeDtypeStruct((N, ROUTER_LANES), jnp.int32), jax.ShapeDtypeStruct((N, ROUTER_LANES), F32),
                   jax.ShapeDtypeStruct((1, ROUTER_LANES), F32)),
        scratch_shapes=[pltpu.VMEM((1, ROUTER_LANES), F32)],
        compiler_params=_params("arbitrary"),
        name="merge_route",
    )(xp, pop, sbp, xs, pos, sbs, norm_mix, w_gates, w_pool_out, w_sb_out, w_o, norm_ffn, w_router, b_router, lower)


def _row_copy(src_ref, src_row, dst_ref, dst_row, sem):
    return pltpu.make_async_copy(src_ref.at[src_row], dst_ref.at[dst_row], sem)


def _slots(pstart_ref, route_ref, r):
    return (pstart_ref[route_ref[0, 4 * r]] + route_ref[0, 4 * r + 2],
            pstart_ref[route_ref[0, 4 * r + 1]] + route_ref[0, 4 * r + 3])


def _dispatch_body(pend_ref, padded_ref, pstart_ref, route_ref, x_ref, buf_ref, zero_ref, sem, zsem, *, tm, bm):
    @pl.when(pl.program_id(0) == 0)
    def _():
        zero_ref[...] = jnp.zeros_like(zero_ref)
        for e in range(N_EXPERTS):
            @pl.when(padded_ref[e] > 0)
            def _():
                cp = pltpu.make_async_copy(zero_ref, buf_ref.at[pl.ds(pend_ref[e] - bm, bm)], zsem)
                cp.start()
                cp.wait()

        def zero_unused_block(bi, c):
            cp = pltpu.make_async_copy(zero_ref, buf_ref.at[pl.ds(bi * bm, bm)], zsem)
            cp.start()
            cp.wait()
            return c

        lax.fori_loop(lax.div(pend_ref[N_EXPERTS - 1], bm), buf_ref.shape[0] // bm, zero_unused_block, 0)

    def issue(j, c):
        for u in range(ROW_DMA_UNROLL):
            r = j * ROW_DMA_UNROLL + u
            d0, d1 = _slots(pstart_ref, route_ref, r)
            _row_copy(x_ref, r, buf_ref, d0, sem).start()
            _row_copy(x_ref, r, buf_ref, d1, sem).start()
        return c

    lax.fori_loop(0, tm // ROW_DMA_UNROLL, issue, 0)

    def drain(j, c):
        for _ in range(2 * ROW_DMA_UNROLL):
            _row_copy(x_ref, 0, buf_ref, 0, sem).wait()
        return c

    lax.fori_loop(0, tm // ROW_DMA_UNROLL, drain, 0)


def _dispatch(hn2, route, pend, padded, pstart, cap, bm, tm):
    N, C, W = hn2.shape
    grid_spec = pltpu.PrefetchScalarGridSpec(
        num_scalar_prefetch=3,
        grid=(N // tm,),
        in_specs=[pl.BlockSpec((None, 1, 4 * tm), lambda i, *_: (i, 0, 0), memory_space=pltpu.SMEM),
                  pl.BlockSpec((tm, C, W), lambda i, *_: (i, 0, 0))],
        out_specs=pl.BlockSpec(memory_space=pl.ANY),
        scratch_shapes=[pltpu.VMEM((bm, C, W), F32), pltpu.SemaphoreType.DMA, pltpu.SemaphoreType.DMA],
    )
    return pl.pallas_call(
        functools.partial(_dispatch_body, tm=tm, bm=bm),
        grid_spec=grid_spec,
        out_shape=jax.ShapeDtypeStruct((cap, C, W), F32),
        compiler_params=_params("arbitrary"),
        name="dispatch",
    )(pend, padded, pstart, route, hn2)


def _expert_body(blk_e_ref, nact_ref, x_ref, wg_ref, wu_ref, wd_ref, y_ref, wg_bf, wu_bf, wd_bf):
    i = pl.program_id(0)

    @pl.when(i < nact_ref[0])
    def _():
        @pl.when((i == 0) | (blk_e_ref[i] != blk_e_ref[jnp.maximum(i - 1, 0)]))
        def _():
            wg_bf[...] = wg_ref[...].astype(BF16)
            wu_bf[...] = wu_ref[...].astype(BF16)
            wd_bf[...] = wd_ref[...].astype(BF16)

        x = _load_rows(x_ref).astype(BF16)
        gate = jnp.dot(x, wg_bf[...], preferred_element_type=F32)
        up = jnp.dot(x, wu_bf[...], preferred_element_type=F32)
        hidden = (jax.nn.silu(gate) * up).astype(BF16)
        _store_rows(y_ref, jnp.dot(hidden, wd_bf[...], preferred_element_type=F32))

    @pl.when(pl.program_id(0) >= nact_ref[0])
    def _():
        y_ref[...] = jnp.zeros_like(y_ref)


def _experts(buf, blk_e, nact, w_gate, w_up, w_down, bm):
    cap, C, W = buf.shape
    D, DE = w_gate.shape[1], w_gate.shape[2]
    rows = lambda i, be, na: (jnp.minimum(i, na[0] - 1), 0, 0)
    grid_spec = pltpu.PrefetchScalarGridSpec(
        num_scalar_prefetch=2,
        grid=(cap // bm,),
        in_specs=[pl.BlockSpec((bm, C, W), rows),
                  pl.BlockSpec((None, D, DE), lambda i, be, na: (be[i], 0, 0)),
                  pl.BlockSpec((None, D, DE), lambda i, be, na: (be[i], 0, 0)),
                  pl.BlockSpec((None, DE, D), lambda i, be, na: (be[i], 0, 0))],
        out_specs=pl.BlockSpec((bm, C, W), lambda i, be, na: (i, 0, 0)),
        scratch_shapes=[pltpu.VMEM((D, DE), BF16), pltpu.VMEM((D, DE), BF16), pltpu.VMEM((DE, D), BF16)],
    )
    return pl.pallas_call(
        _expert_body,
        grid_spec=grid_spec,
        out_shape=jax.ShapeDtypeStruct((cap, C, W), F32),
        compiler_params=_params("arbitrary"),
        name="experts",
    )(blk_e, nact, buf, w_gate, w_up, w_down)


def _combine_body(pstart_ref, route_ref, h2_ref, rw_ref, yb_ref, yp_ref, ys_ref, rows_ref, sem, *, tm, prompt_tiles):
    def issue(j, c):
        for u in range(ROW_DMA_UNROLL):
            r = j * ROW_DMA_UNROLL + u
            d0, d1 = _slots(pstart_ref, route_ref, r)
            _row_copy(yb_ref, d0, rows_ref.at[0], r, sem).start()
            _row_copy(yb_ref, d1, rows_ref.at[1], r, sem).start()
        return c

    lax.fori_loop(0, tm // ROW_DMA_UNROLL, issue, 0)

    def drain(j, c):
        for _ in range(2 * ROW_DMA_UNROLL):
            _row_copy(yb_ref, 0, rows_ref.at[0], 0, sem).wait()
        return c

    lax.fori_loop(0, tm // ROW_DMA_UNROLL, drain, 0)
    w = rw_ref[...]
    y = h2_ref[...] + (_load_rows(rows_ref.at[0]) * w[:, 0:1] + _load_rows(rows_ref.at[1]) * w[:, 1:2])

    @pl.when(pl.program_id(0) < prompt_tiles)
    def _():
        yp_ref[...] = y

    @pl.when(pl.program_id(0) >= prompt_tiles)
    def _():
        ys_ref[...] = y


def _combine(h2, rw, route, pstart, yb, tm, n_prompt):
    N, D = h2.shape
    C, W = yb.shape[1:]
    tp = n_prompt // tm
    grid_spec = pltpu.PrefetchScalarGridSpec(
        num_scalar_prefetch=1,
        grid=(N // tm,),
        in_specs=[pl.BlockSpec((None, 1, 4 * tm), lambda i, ps: (i, 0, 0), memory_space=pltpu.SMEM),
                  pl.BlockSpec((tm, D), lambda i, ps: (i, 0)),
                  pl.BlockSpec((tm, ROUTER_LANES), lambda i, ps: (i, 0)),
                  pl.BlockSpec(memory_space=pl.ANY)],
        out_specs=[pl.BlockSpec((tm, D), lambda i, ps: (jnp.minimum(i, tp - 1), 0)),
                   pl.BlockSpec((tm, D), lambda i, ps: (jnp.maximum(i - tp, 0), 0))],
        scratch_shapes=[pltpu.VMEM((2, tm, C, W), F32), pltpu.SemaphoreType.DMA],
    )
    return pl.pallas_call(
        functools.partial(_combine_body, tm=tm, prompt_tiles=tp),
        grid_spec=grid_spec,
        out_shape=(jax.ShapeDtypeStruct((n_prompt, D), F32), jax.ShapeDtypeStruct((N - n_prompt, D), F32)),
        compiler_params=_params("arbitrary"),
        name="combine",
    )(pstart, route, h2, rw, yb)


def _moe(h2, hn2, ri, rw, counts, w_gate, w_up, w_down, tm, n_prompt, bm):
    N, D = h2.shape
    cnt = counts[0, N_GROUPS:N_GROUPS + N_EXPERTS].astype(jnp.int32)
    padded = (cnt + bm - 1) // bm * bm
    pend = jnp.cumsum(padded).astype(jnp.int32)
    pstart = pend - padded
    route = ri[:, 0:4].reshape(N // tm, 1, 4 * tm)
    cap = -(-(2 * N + N_EXPERTS * (bm - 1)) // bm) * bm
    nblk = cap // bm
    starts = jnp.arange(nblk, dtype=jnp.int32)[:, None] * bm
    blk_e = jnp.minimum(jnp.sum((pend[None, :] <= starts).astype(jnp.int32), axis=1), N_EXPERTS - 1)
    nact = pend[-1:] // bm
    buf = _dispatch(hn2, route, pend, padded, pstart, cap, bm, tm)
    yb = _experts(buf, blk_e, nact, w_gate, w_up, w_down, bm)
    return _combine(h2, rw, route, pstart, yb, tm, n_prompt)


def _constants(tm):
    r = lax.broadcasted_iota(jnp.int32, (KEY_BLOCK, KEY_BLOCK), 0)
    c = lax.broadcasted_iota(jnp.int32, (KEY_BLOCK, KEY_BLOCK), 1)
    tri = jnp.where(r > c, -1.0, 0.0).astype(BF16)
    hr = lax.broadcasted_iota(jnp.int32, (SB_WIDTH, SB_WIDTH), 0) // SB_HEAD_DIM
    hc = lax.broadcasted_iota(jnp.int32, (SB_WIDTH, SB_WIDTH), 1) // SB_HEAD_DIM
    hsum = jnp.where(hr == hc, 1.0 / SB_HEAD_DIM, 0.0).astype(BF16)
    lr = lax.broadcasted_iota(jnp.int32, (tm, tm), 0)
    lc = lax.broadcasted_iota(jnp.int32, (tm, tm), 1)
    lower = jnp.where(lc < lr, 1.0, 0.0).astype(BF16)
    return tri, hsum, lower


def kernel(x_prompt, x_sample, cache_sb_k, cache_sb_v, state_pool, meta_tokens, norm_mix, w_in, q_norm, k_norm,
           pool_mix, pool_scale, w_pool_out, w_sb_out, w_o, norm_ffn, w_router_group, b_router_group,
           w_router_expert, b_router_expert, w_gate, w_up, w_down):
    assert norm_mix.shape[0] == 1, "single-layer trunk"
    B, S, D = x_prompt.shape
    BS, SS, _ = x_sample.shape
    past = cache_sb_k.shape[2]
    tm = min(512, B * S, BS * SS)
    tri, hsum, lower = _constants(tm)

    w_in0 = w_in[0]
    w_uqkv = w_in0[:, :4 * SB_WIDTH].astype(BF16)
    w_gates = w_in0[:, 4 * SB_WIDTH:].astype(BF16)
    nm = norm_mix[0][None, :]
    nf = norm_ffn[0][None, :]
    qg = jnp.tile(q_norm[0], SB_HEADS)[None, :]
    kg = jnp.tile(k_norm[0], SB_HEADS)[None, :]
    pm = pool_mix[0].astype(BF16)
    ps = pool_scale[0][None, :]
    wpo = w_pool_out[0].astype(BF16)
    wsb = w_sb_out[0].astype(BF16)
    wo = w_o[0].astype(BF16)
    pad = ROUTER_LANES - N_GROUPS - N_EXPERTS
    w_router = jnp.concatenate([w_router_group[0], w_router_expert[0], jnp.zeros((D, pad), F32)], axis=1).astype(BF16)
    b_router = jnp.concatenate([b_router_group[0], b_router_expert[0], jnp.zeros((pad,), F32)])[None, :]

    inproj = functools.partial(_inproj, norm_mix=nm, w_uqkv=w_uqkv, qg=qg, kg=kg, hsum=hsum, pool_mix=pm,
                               pool_scale=ps)

    shape5 = lambda a: a.reshape(1, a.shape[0], a.shape[1], SB_HEADS, SB_HEAD_DIM)
    flat = lambda a: a.reshape(a.shape[0] * a.shape[1], a.shape[2])

    prev_s = jnp.concatenate([jnp.zeros((BS, CARRY_ROWS - POOL_STATE, POOL_WIDTH), F32), state_pool[0]], axis=1)
    q_s, k_s, v_s, ks_f, vs_f, po_s, utail_s = inproj(x_sample, prev_s)
    kc = cache_sb_k[0].astype(BF16).reshape(BS, past, SB_WIDTH)
    vc = cache_sb_v[0].astype(BF16).reshape(BS, past, SB_WIDTH)
    sb_s = _attention(q_s, k_s, v_s, kc, vc, tri)
    new_pool_sample = utail_s[None, :, CARRY_ROWS - POOL_STATE:, :]

    zeros_prev = jnp.zeros((1, CARRY_ROWS, POOL_WIDTH), F32)
    _, km_b, vm_b, km_f, vm_f, _, u_meta = inproj(meta_tokens[None], zeros_prev)

    q_b, k_b, v_b, k_f, v_f, po, utail = inproj(x_prompt, u_meta, k_lead=km_f, v_lead=vm_f)
    sb = _attention(q_b, k_b, v_b, km_b, vm_b, tri)
    new_k_prompt, new_v_prompt = shape5(k_f), shape5(v_f)
    new_pool_prompt = utail[None, :, CARRY_ROWS - POOL_STATE:, :]

    h2, hn2, ri, rw, counts = _merge_route((flat(x_prompt), flat(po), flat(sb)), (flat(x_sample), flat(po_s), flat(sb_s)),
                                           nm, w_gates, wpo, wsb, wo, nf, w_router, b_router, lower)
    y_p, y_s = _moe(h2, hn2, ri, rw, counts, w_gate[0], w_up[0], w_down[0], tm, B * S, MOE_BLOCK_ROWS)
    y_prompt, y_sample = y_p.reshape(x_prompt.shape), y_s.reshape(x_sample.shape)

    return (y_prompt, y_sample, new_k_prompt, new_v_prompt, new_pool_prompt,
            shape5(ks_f), shape5(vs_f), new_pool_sample)
```

```python
import functools

import jax
import jax.numpy as jnp
from jax import lax
from jax.experimental import pallas as pl
from jax.experimental.pallas import tpu as pltpu

F32 = jnp.float32
BF16 = jnp.bfloat16

N_META = 16
POOL_WINDOWS = (2, 4, 8, 16)
POOL_GROUP = 128
POOL_WIDTH = 512
POOL_STATE = 15
SB_HEADS = 8
SB_HEAD_DIM = 64
SB_WIDTH = 512
HEAD_PAIR_LANES = 2 * SB_HEAD_DIM
N_GROUPS = 4
EXPERTS_PER_GROUP = 8
N_EXPERTS = 32
ROUTER_LANES = 128
MOE_BLOCK_ROWS = 512
ROW_TILE = (8, 128)
ROW_DMA_UNROLL = 8
NORM_EPS = 1e-6
CARRY_ROWS = 16
KEY_BLOCK = 256
VMEM_LIMIT_BYTES = 56 * 1024 * 1024


def _params(*sem):
    return pltpu.CompilerParams(dimension_semantics=sem, vmem_limit_bytes=VMEM_LIMIT_BYTES)


def _rms(x, gain):
    ms = jnp.mean(x * x, axis=-1, keepdims=True)
    return x * lax.rsqrt(ms + NORM_EPS) * gain


def _inproj_body(x_ref, nm_ref, w_ref, qg_ref, kg_ref, hs_ref, pm_ref, ps_ref, uprev_ref, klead_ref, vlead_ref,
                 qb_ref, kb_ref, vb_ref, kf_ref, vf_ref, po_ref, ut_ref, carry_ref, kcarry_ref, vcarry_ref,
                 *, tm, n_tiles, lead):
    t = pl.program_id(1)

    @pl.when(t == 0)
    def _():
        carry_ref[...] = uprev_ref[...]
        if lead:
            kcarry_ref[...] = klead_ref[...]
            vcarry_ref[...] = vlead_ref[...]

    if lead:
        @pl.when(t == n_tiles)
        def _():
            kf_ref[0:lead, :] = kcarry_ref[...]
            vf_ref[0:lead, :] = vcarry_ref[...]

    @pl.when(t < n_tiles)
    def _():
        _inproj_tile(x_ref, nm_ref, w_ref, qg_ref, kg_ref, hs_ref, pm_ref, ps_ref,
                     qb_ref, kb_ref, vb_ref, kf_ref, vf_ref, po_ref, ut_ref, carry_ref, kcarry_ref, vcarry_ref,
                     tm=tm, lead=lead)


def _inproj_tile(x_ref, nm_ref, w_ref, qg_ref, kg_ref, hs_ref, pm_ref, ps_ref,
                 qb_ref, kb_ref, vb_ref, kf_ref, vf_ref, po_ref, ut_ref, carry_ref, kcarry_ref, vcarry_ref, *, tm, lead):
    hn = _rms(x_ref[...], nm_ref[...]).astype(BF16)
    z = jnp.dot(hn, w_ref[...], preferred_element_type=F32)
    u = z[:, 0:POOL_WIDTH]
    v = z[:, 3 * SB_WIDTH:4 * SB_WIDTH]

    def head_norm(a, gain):
        ms = jnp.dot((a * a).astype(BF16), hs_ref[...], preferred_element_type=F32)
        return a * lax.rsqrt(ms + NORM_EPS) * gain

    qn = head_norm(z[:, SB_WIDTH:2 * SB_WIDTH], qg_ref[...])
    kn = head_norm(z[:, 2 * SB_WIDTH:3 * SB_WIDTH], kg_ref[...])
    qb_ref[...] = (qn * (SB_HEAD_DIM ** -0.5)).astype(BF16)
    kb_ref[...] = kn.astype(BF16)
    vb_ref[...] = v.astype(BF16)
    if lead:
        kf_ref[...] = jnp.concatenate([kcarry_ref[...], kn[:tm - lead]], axis=0)
        vf_ref[...] = jnp.concatenate([vcarry_ref[...], v[:tm - lead]], axis=0)
        kcarry_ref[...] = kn[tm - lead:]
        vcarry_ref[...] = v[tm - lead:]
    else:
        kf_ref[...] = kn
        vf_ref[...] = v

    ext = jnp.concatenate([carry_ref[...], u], axis=0)
    mixed = []
    for gi, w in enumerate(POOL_WINDOWS):
        c0, c1 = gi * POOL_GROUP, (gi + 1) * POOL_GROUP
        s = ext[:, c0:c1]
        shift = 1
        while shift < w:
            s = s + pltpu.roll(s, shift, axis=0)
            shift *= 2
        pooled = s[CARRY_ROWS:, :] * (1.0 / w) - u[:, c0:c1]
        mixed.append(jnp.dot(pooled.astype(BF16), pm_ref[gi], preferred_element_type=F32))
    po_ref[...] = (jnp.concatenate(mixed, axis=1) * ps_ref[...]).astype(BF16)
    carry_ref[...] = u[tm - CARRY_ROWS:, :]
    ut_ref[...] = u[tm - CARRY_ROWS:, :]


def _inproj(x, uprev, norm_mix, w_uqkv, qg, kg, hsum, pool_mix, pool_scale, k_lead=None, v_lead=None):
    B, S, D = x.shape
    tm = min(512, S)
    n_tiles = S // tm
    lead = 0 if k_lead is None else k_lead.shape[1]
    assert S % tm == 0 and tm % CARRY_ROWS == 0 and lead % ROW_TILE[0] == 0 and lead < tm
    if k_lead is None:
        k_lead = v_lead = jnp.zeros((1, ROW_TILE[0], SB_WIDTH), F32)
    per_batch_prev = uprev.shape[0] == B and B > 1
    last = n_tiles - 1
    row = lambda b, t: (b, jnp.minimum(t, last), 0)
    late = lambda b, t: (b, t, 0)
    const2 = lambda b, t: (0, 0)
    out_shape = (
        jax.ShapeDtypeStruct((B, S, SB_WIDTH), BF16),
        jax.ShapeDtypeStruct((B, S, SB_WIDTH), BF16),
        jax.ShapeDtypeStruct((B, S, SB_WIDTH), BF16),
        jax.ShapeDtypeStruct((B, S + lead, SB_WIDTH), F32),
        jax.ShapeDtypeStruct((B, S + lead, SB_WIDTH), F32),
        jax.ShapeDtypeStruct((B, S, POOL_WIDTH), BF16),
        jax.ShapeDtypeStruct((B, CARRY_ROWS, POOL_WIDTH), F32),
    )
    blk = lambda w: pl.BlockSpec((None, tm, w), row)
    return pl.pallas_call(
        functools.partial(_inproj_body, tm=tm, n_tiles=n_tiles, lead=lead),
        grid=(B, n_tiles + (1 if lead else 0)),
        in_specs=[
            pl.BlockSpec((None, tm, D), row),
            pl.BlockSpec((1, D), const2),
            pl.BlockSpec((D, 4 * SB_WIDTH), const2),
            pl.BlockSpec((1, SB_WIDTH), const2),
            pl.BlockSpec((1, SB_WIDTH), const2),
            pl.BlockSpec((SB_WIDTH, SB_WIDTH), const2),
            pl.BlockSpec((len(POOL_WINDOWS), POOL_GROUP, POOL_GROUP), lambda b, t: (0, 0, 0)),
            pl.BlockSpec((1, POOL_WIDTH), const2),
            pl.BlockSpec((None, CARRY_ROWS, POOL_WIDTH),
                         (lambda b, t: (b, 0, 0)) if per_batch_prev else (lambda b, t: (0, 0, 0))),
            pl.BlockSpec((None,) + k_lead.shape[1:], lambda b, t: (0, 0, 0)),
            pl.BlockSpec((None,) + v_lead.shape[1:], lambda b, t: (0, 0, 0)),
        ],
        out_specs=[blk(SB_WIDTH), blk(SB_WIDTH), blk(SB_WIDTH),
                   pl.BlockSpec((None, tm, SB_WIDTH), late), pl.BlockSpec((None, tm, SB_WIDTH), late), blk(POOL_WIDTH),
                   pl.BlockSpec((None, CARRY_ROWS, POOL_WIDTH), lambda b, t: (b, 0, 0))],
        out_shape=out_shape,
        scratch_shapes=[pltpu.VMEM((CARRY_ROWS, POOL_WIDTH), F32),
                        pltpu.VMEM(k_lead.shape[1:], F32), pltpu.VMEM(v_lead.shape[1:], F32)],
        compiler_params=_params("arbitrary", "arbitrary"),
        name="inproj",
    )(x, norm_mix, w_uqkv, qg, kg, hsum, pool_mix, pool_scale, uprev, k_lead, v_lead)


def _log_gates(z):
    sign = jnp.uint32(0x80000000)
    neg_abs = lax.bitcast_convert_type(lax.bitcast_convert_type(z, jnp.uint32) | sign, F32)
    sp = jnp.maximum(z, 0.0) + jnp.log(1.0 + jnp.exp(neg_abs))
    return sp, z - sp


def _dot_nt(a, b):
    return lax.dot_general(a, b, (((1,), (1,)), ((), ())), preferred_element_type=F32)


def _attn_body(vq_ref, vk_ref, q_ref, k_ref, v_ref, kp_ref, vp_ref, tri_ref, o_ref,
               z_scr, e_scr, acc_scr, run_scr, *, L, P, tq, n_visits, full_from_prefix, keys_on_lanes):
    nq = L // tq
    lane = lax.broadcasted_iota(jnp.int32, (1, HEAD_PAIR_LANES), 1)
    heads = (lane < SB_HEAD_DIM, lane >= SB_HEAD_DIM)
    ksrc, vsrc = (kp_ref, vp_ref) if full_from_prefix else (k_ref, v_ref)

    def stacked_q(qi):
        qp = q_ref[pl.ds(pl.multiple_of(qi * tq, tq), tq), :]
        zero = jnp.zeros((), BF16)
        return jnp.concatenate([jnp.where(heads[0], qp, zero), jnp.where(heads[1], qp, zero)], axis=0)

    def key_rows(s):
        return pl.ds(pl.multiple_of(vk_ref[s] * KEY_BLOCK, KEY_BLOCK), KEY_BLOCK)

    def stage_a(s):
        if keys_on_lanes:
            z_scr[...] = jnp.dot(stacked_q(vq_ref[s]), ksrc[:, key_rows(s)].astype(BF16), preferred_element_type=F32)
        else:
            z_scr[...] = _dot_nt(stacked_q(vq_ref[s]), ksrc[key_rows(s), :].astype(BF16))

    def stage_b(s):
        qi = vq_ref[s]
        sp, ls = _log_gates(z_scr[...])
        later = jnp.dot(sp.astype(BF16), tri_ref[...], preferred_element_type=F32)
        run = run_scr[qi]
        e_scr[...] = ls + (later + jnp.concatenate([run, run], axis=1))
        run_scr[qi] = run - jnp.sum(sp, axis=1, keepdims=True)

    def stage_c(s):
        qi = vq_ref[s]
        a = jnp.exp(e_scr[...]).astype(BF16)
        if keys_on_lanes:
            acc_scr[qi] = acc_scr[qi] + _dot_nt(a, vsrc[:, key_rows(s)].astype(BF16))
        else:
            acc_scr[qi] = acc_scr[qi] + jnp.dot(a, vsrc[key_rows(s), :].astype(BF16), preferred_element_type=F32)

    if n_visits:
        acc_scr[...] = jnp.zeros_like(acc_scr)
        run_scr[...] = jnp.zeros_like(run_scr)
        stage_a(0)
        if n_visits >= 2:
            stage_b(0)
            stage_a(1)

        def steady(s, c):
            stage_c(s - 2)
            stage_b(s - 1)
            stage_a(s)
            return c

        lax.fori_loop(2, n_visits, steady, 0)
        if n_visits >= 2:
            stage_c(n_visits - 2)
        stage_b(n_visits - 1)
        stage_c(n_visits - 1)

    rows = lax.broadcasted_iota(jnp.int32, (2 * tq, tq), 0)
    cols = lax.broadcasted_iota(jnp.int32, (2 * tq, tq), 1)
    visible = (cols < rows) & (rows < tq) | (cols < rows - tq)

    def q_block(qi):
        qrows = pl.ds(pl.multiple_of(qi * tq, tq), tq)
        q2 = stacked_q(qi)
        sp, ls = _log_gates(_dot_nt(q2, k_ref[qrows, :]))
        sp = jnp.where(visible, sp, 0.0)
        later = jnp.dot(sp.astype(BF16), tri_ref[0:tq, 0:tq], preferred_element_type=F32)
        a = jnp.where(visible, jnp.exp(ls + later), 0.0)
        acc = jnp.dot(a.astype(BF16), v_ref[qrows, :], preferred_element_type=F32)
        run = -jnp.sum(sp, axis=1, keepdims=True)
        if n_visits:
            acc = acc + jnp.exp(run) * acc_scr[qi]
            run = run + run_scr[qi][:, 0:1]
        if not full_from_prefix:
            sp, ls = _log_gates(_dot_nt(q2, kp_ref[...].astype(BF16)))
            later = jnp.dot(sp.astype(BF16), tri_ref[0:P, 0:P], preferred_element_type=F32)
            a = jnp.exp(ls + (later + run))
            acc = acc + jnp.dot(a.astype(BF16), vp_ref[...].astype(BF16), preferred_element_type=F32)
        o_ref[qrows, :] = jnp.where(heads[0], acc[:tq], acc[tq:]).astype(BF16)

    def q_block_pair(j, carry):
        q_block(2 * j)
        q_block(2 * j + 1)
        return carry

    lax.fori_loop(0, nq // 2, q_block_pair, 0)
    if nq % 2:
        q_block(nq - 1)


def _attention(q, k, v, kp, vp, tri, keys_on_lanes=False):
    B, L, _ = q.shape
    P = kp.shape[2] if keys_on_lanes else kp.shape[1]
    tq = min(KEY_BLOCK, L)
    nq = L // tq
    assert L % tq == 0 and tq % 16 == 0 and P % 16 == 0
    full_from_prefix = P > KEY_BLOCK
    if full_from_prefix:
        assert nq == 1 and P % KEY_BLOCK == 0
        visits = [(qi, kb) for qi in range(nq) for kb in reversed(range(P // KEY_BLOCK))]
    else:
        assert tq == KEY_BLOCK or nq == 1
        visits = [(qi, kb) for qi in range(nq) for kb in reversed(range(qi))]
    n_visits = len(visits)
    vq = jnp.asarray([qi for qi, _ in visits] or [0], jnp.int32)
    vk = jnp.asarray([kb for _, kb in visits] or [0], jnp.int32)
    per_batch_prefix = kp.shape[0] == B and B > 1
    seq = pl.BlockSpec((None, L, HEAD_PAIR_LANES), lambda b, hp, *_: (b, 0, hp))
    if keys_on_lanes:
        assert full_from_prefix and per_batch_prefix
        pre = pl.BlockSpec((None, HEAD_PAIR_LANES, P), lambda b, hp, *_: (b, hp, 0))
    else:
        pre = pl.BlockSpec((None, P, HEAD_PAIR_LANES),
                           (lambda b, hp, *_: (b, 0, hp)) if per_batch_prefix else (lambda b, hp, *_: (0, 0, hp)))
    grid_spec = pltpu.PrefetchScalarGridSpec(
        num_scalar_prefetch=2,
        grid=(B, SB_WIDTH // HEAD_PAIR_LANES),
        in_specs=[seq, seq, seq, pre, pre, pl.BlockSpec((KEY_BLOCK, KEY_BLOCK), lambda b, hp, *_: (0, 0))],
        out_specs=seq,
        scratch_shapes=[pltpu.VMEM((2 * tq, KEY_BLOCK), F32), pltpu.VMEM((2 * tq, KEY_BLOCK), F32),
                        pltpu.VMEM((nq, 2 * tq, HEAD_PAIR_LANES), F32),
                        pltpu.VMEM((nq, 2 * tq, HEAD_PAIR_LANES), F32)],
    )
    return pl.pallas_call(
        functools.partial(_attn_body, L=L, P=P, tq=tq, n_visits=n_visits, full_from_prefix=full_from_prefix,
                          keys_on_lanes=keys_on_lanes),
        grid_spec=grid_spec,
        out_shape=jax.ShapeDtypeStruct((B, L, SB_WIDTH), BF16),
        compiler_params=_params("arbitrary", "arbitrary"),
        name="stickbreak",
    )(vq, vk, q, k, v, kp, vp, tri)


def _transpose8(v):
    sub = lax.broadcasted_iota(jnp.int32, (1, ROW_TILE[0], 1), 1)
    for k in (4, 2, 1):
        low = (sub & k) == 0
        nxt = list(v)
        for i in range(ROW_TILE[0]):
            if i & k == 0:
                j = i | k
                nxt[i] = jnp.where(low, v[i], pltpu.roll(v[j], k, axis=1))
                nxt[j] = jnp.where(low, pltpu.roll(v[i], ROW_TILE[0] - k, axis=1), v[j])
        v = nxt
    return v


def _store_rows(ref, val):
    s, w = ROW_TILE
    g = val.shape[0] // s
    v = _transpose8([val[:, c * w:(c + 1) * w].reshape(g, s, w) for c in range(s)])
    for r in range(s):
        ref[pl.ds(r, g, stride=s), :, :] = v[r]


def _load_rows(ref):
    s, w = ROW_TILE
    g = ref.shape[0] // s
    v = _transpose8([ref[pl.ds(r, g, stride=s), :, :] for r in range(s)])
    return jnp.concatenate([x.reshape(g * s, w) for x in v], axis=1)


def _merge_body(xp_ref, pop_ref, sbp_ref, xs_ref, pos_ref, sbs_ref,
                nm_ref, wg_ref, wpo_ref, wsb_ref, wo_ref, nf_ref, wr_ref, br_ref, low_ref,
                h2_ref, hn2_ref, ri_ref, rw_ref, cnt_ref, carry_ref, *, prompt_tiles):
    @pl.when(pl.program_id(0) == 0)
    def _():
        carry_ref[...] = jnp.zeros_like(carry_ref)

    from_prompt = pl.program_id(0) < prompt_tiles
    x = jnp.where(from_prompt, xp_ref[...], xs_ref[...])
    po = jnp.where(from_prompt, pop_ref[...], pos_ref[...])
    sb = jnp.where(from_prompt, sbp_ref[...], sbs_ref[...])
    D = x.shape[1]
    hn = _rms(x, nm_ref[...]).astype(BF16)
    gates = jnp.dot(hn, wg_ref[...], preferred_element_type=F32)
    pool_proj = jnp.dot(po, wpo_ref[...], preferred_element_type=F32)
    sb_proj = jnp.dot(sb, wsb_ref[...], preferred_element_type=F32)
    merged = jax.nn.sigmoid(gates[:, :D]) * pool_proj + jax.nn.sigmoid(gates[:, D:]) * sb_proj
    h2 = x + jnp.dot(merged.astype(BF16), wo_ref[...], preferred_element_type=F32)
    h2_ref[...] = h2
    hn2 = _rms(h2, nf_ref[...])
    _store_rows(hn2_ref, hn2)

    logits = jnp.dot(hn2.astype(BF16), wr_ref[...], preferred_element_type=F32) + br_ref[...]
    lane = lax.broadcasted_iota(jnp.int32, logits.shape, 1)
    lanef = lane.astype(F32)
    neg = jnp.float32(-jnp.inf)
    big = jnp.float32(2 * ROUTER_LANES)

    def first_argmax(vals):
        m = jnp.max(vals, axis=1, keepdims=True)
        return m, jnp.min(jnp.where(vals == m, lanef, big), axis=1, keepdims=True)

    is_group = lane < N_GROUPS
    gmax, gidx = first_argmax(jnp.where(is_group, logits, neg))
    p_group = 1.0 / jnp.sum(jnp.where(is_group, jnp.exp(logits - gmax), 0.0), axis=1, keepdims=True)
    first = N_GROUPS + EXPERTS_PER_GROUP * gidx
    in_group = (lanef >= first) & (lanef < first + EXPERTS_PER_GROUP)
    cand = jnp.where(in_group, logits, neg)
    m1, i1 = first_argmax(cand)
    m2, i2 = first_argmax(jnp.where(lanef == i1, neg, cand))
    t = jnp.exp(m2 - m1)
    w1 = p_group / (1.0 + t)
    w2 = p_group * t / (1.0 + t)

    hit1 = lanef == i1
    hit2 = lanef == i2
    onehot = jnp.where(hit1 | hit2, 1.0, 0.0)
    before = jnp.dot(low_ref[...], onehot.astype(BF16), preferred_element_type=F32) + carry_ref[...]
    r1 = jnp.sum(jnp.where(hit1, before, 0.0), axis=1, keepdims=True)
    r2 = jnp.sum(jnp.where(hit2, before, 0.0), axis=1, keepdims=True)
    carry_ref[...] = carry_ref[...] + jnp.sum(onehot, axis=0, keepdims=True)
    cnt_ref[...] = carry_ref[...]

    slab = jnp.where(lane == 0, i1 - N_GROUPS,
                     jnp.where(lane == 1, i2 - N_GROUPS, jnp.where(lane == 2, r1, jnp.where(lane == 3, r2, 0.0))))
    ri_ref[...] = slab.astype(jnp.int32)
    rw_ref[...] = jnp.where(lane == 0, w1, jnp.where(lane == 1, w2, 0.0))


def _merge_route(prompt, sample, norm_mix, w_gates, w_pool_out, w_sb_out, w_o, norm_ffn, w_router, b_router, lower):
    tm = lower.shape[0]
    (xp, pop, sbp), (xs, pos, sbs) = prompt, sample
    D = xp.shape[1]
    assert xp.shape[0] % tm == 0 and xs.shape[0] % tm == 0 and D == ROW_TILE[0] * ROW_TILE[1]
    tp, ts = xp.shape[0] // tm, xs.shape[0] // tm
    N = xp.shape[0] + xs.shape[0]
    from_p = lambda i: (jnp.minimum(i, tp - 1), 0)
    from_s = lambda i: (jnp.maximum(i - tp, 0), 0)
    row = lambda i: (i, 0)
    const = lambda i: (0, 0)
    full = lambda a: pl.BlockSpec(a.shape, const)
    group = lambda m: [pl.BlockSpec((tm, D), m), pl.BlockSpec((tm, POOL_WIDTH), m), pl.BlockSpec((tm, SB_WIDTH), m)]
    return pl.pallas_call(
        functools.partial(_merge_body, prompt_tiles=tp),
        grid=(tp + ts,),
        in_specs=group(from_p) + group(from_s) + [
            full(norm_mix), full(w_gates), full(w_pool_out), full(w_sb_out), full(w_o), full(norm_ffn),
            full(w_router), full(b_router), full(lower)],
        out_specs=[pl.BlockSpec((tm, D), row), pl.BlockSpec((tm,) + ROW_TILE, lambda i: (i, 0, 0)),
                   pl.BlockSpec((tm, ROUTER_LANES), row), pl.BlockSpec((tm, ROUTER_LANES), row),
                   pl.BlockSpec((1, ROUTER_LANES), const)],
        out_shape=(jax.ShapeDtypeStruct((N, D), F32), jax.ShapeDtypeStruct((N,) + ROW_TILE, F32),
                   jax.ShapeDtypeStruct((N, ROUTER_LANES), jnp.int32), jax.ShapeDtypeStruct((N, ROUTER_LANES), F32),
                   jax.ShapeDtypeStruct((1, ROUTER_LANES), F32)),
        scratch_shapes=[pltpu.VMEM((1, ROUTER_LANES), F32)],
        compiler_params=_params("arbitrary"),
        name="merge_route",
    )(xp, pop, sbp, xs, pos, sbs, norm_mix, w_gates, w_pool_out, w_sb_out, w_o, norm_ffn, w_router, b_router, lower)


def _row_copy(src_ref, src_row, dst_ref, dst_row, sem):
    return pltpu.make_async_copy(src_ref.at[src_row], dst_ref.at[dst_row], sem)


def _slots(pstart_ref, route_ref, r):
    return (pstart_ref[route_ref[0, 4 * r]] + route_ref[0, 4 * r + 2],
            pstart_ref[route_ref[0, 4 * r + 1]] + route_ref[0, 4 * r + 3])


def _dispatch_body(pend_ref, padded_ref, pstart_ref, route_ref, x_ref, buf_ref, zero_ref, sem, zsem, *, tm, bm):
    @pl.when(pl.program_id(0) == 0)
    def _():
        zero_ref[...] = jnp.zeros_like(zero_ref)
        for e in range(N_EXPERTS):
            @pl.when(padded_ref[e] > 0)
            def _():
                cp = pltpu.make_async_copy(zero_ref, buf_ref.at[pl.ds(pend_ref[e] - bm, bm)], zsem)
                cp.start()
                cp.wait()

        def zero_unused_block(bi, c):
            cp = pltpu.make_async_copy(zero_ref, buf_ref.at[pl.ds(bi * bm, bm)], zsem)
            cp.start()
            cp.wait()
            return c

        lax.fori_loop(lax.div(pend_ref[N_EXPERTS - 1], bm), buf_ref.shape[0] // bm, zero_unused_block, 0)

    def issue(j, c):
        for u in range(ROW_DMA_UNROLL):
            r = j * ROW_DMA_UNROLL + u
            d0, d1 = _slots(pstart_ref, route_ref, r)
            _row_copy(x_ref, r, buf_ref, d0, sem).start(priority=0)
            _row_copy(x_ref, r, buf_ref, d1, sem).start(priority=1)
        return c

    lax.fori_loop(0, tm // ROW_DMA_UNROLL, issue, 0)

    def drain(j, c):
        for _ in range(2 * ROW_DMA_UNROLL):
            _row_copy(x_ref, 0, buf_ref, 0, sem).wait()
        return c

    lax.fori_loop(0, tm // ROW_DMA_UNROLL, drain, 0)


def _dispatch(hn2, route, pend, padded, pstart, cap, bm, tm):
    N, C, W = hn2.shape
    grid_spec = pltpu.PrefetchScalarGridSpec(
        num_scalar_prefetch=3,
        grid=(N // tm,),
        in_specs=[pl.BlockSpec((None, 1, 4 * tm), lambda i, *_: (i, 0, 0), memory_space=pltpu.SMEM),
                  pl.BlockSpec((tm, C, W), lambda i, *_: (i, 0, 0))],
        out_specs=pl.BlockSpec(memory_space=pl.ANY),
        scratch_shapes=[pltpu.VMEM((bm, C, W), F32), pltpu.SemaphoreType.DMA, pltpu.SemaphoreType.DMA],
    )
    return pl.pallas_call(
        functools.partial(_dispatch_body, tm=tm, bm=bm),
        grid_spec=grid_spec,
        out_shape=jax.ShapeDtypeStruct((cap, C, W), F32),
        compiler_params=_params("arbitrary"),
        name="dispatch",
    )(pend, padded, pstart, route, hn2)


def _expert_body(blk_e_ref, nact_ref, x_ref, wg_ref, wu_ref, wd_ref, y_ref, wg_bf, wu_bf, wd_bf):
    i = pl.program_id(0)

    @pl.when(i < nact_ref[0])
    def _():
        @pl.when((i == 0) | (blk_e_ref[i] != blk_e_ref[jnp.maximum(i - 1, 0)]))
        def _():
            wg_bf[...] = wg_ref[...].astype(BF16)
            wu_bf[...] = wu_ref[...].astype(BF16)
            wd_bf[...] = wd_ref[...].astype(BF16)

        x = _load_rows(x_ref).astype(BF16)
        gate = jnp.dot(x, wg_bf[...], preferred_element_type=F32)
        up = jnp.dot(x, wu_bf[...], preferred_element_type=F32)
        hidden = (jax.nn.silu(gate) * up).astype(BF16)
        _store_rows(y_ref, jnp.dot(hidden, wd_bf[...], preferred_element_type=F32))

    @pl.when(pl.program_id(0) >= nact_ref[0])
    def _():
        y_ref[...] = jnp.zeros_like(y_ref)


def _experts(buf, blk_e, nact, w_gate, w_up, w_down, bm):
    cap, C, W = buf.shape
    D, DE = w_gate.shape[1], w_gate.shape[2]
    rows = lambda i, be, na: (jnp.minimum(i, na[0] - 1), 0, 0)
    grid_spec = pltpu.PrefetchScalarGridSpec(
        num_scalar_prefetch=2,
        grid=(cap // bm,),
        in_specs=[pl.BlockSpec((bm, C, W), rows),
                  pl.BlockSpec((None, D, DE), lambda i, be, na: (be[i], 0, 0)),
                  pl.BlockSpec((None, D, DE), lambda i, be, na: (be[i], 0, 0)),
                  pl.BlockSpec((None, DE, D), lambda i, be, na: (be[i], 0, 0))],
        out_specs=pl.BlockSpec((bm, C, W), lambda i, be, na: (i, 0, 0)),
        scratch_shapes=[pltpu.VMEM((D, DE), BF16), pltpu.VMEM((D, DE), BF16), pltpu.VMEM((DE, D), BF16)],
    )
    return pl.pallas_call(
        _expert_body,
        grid_spec=grid_spec,
        out_shape=jax.ShapeDtypeStruct((cap, C, W), F32),
        compiler_params=_params("arbitrary"),
        name="experts",
    )(blk_e, nact, buf, w_gate, w_up, w_down)


def _combine_body(pstart_ref, route_ref, nxt_ref, h2_ref, rw_ref, yb_ref, yp_ref, ys_ref, rows_ref, sems,
                  *, tm, prompt_tiles, n_tiles):
    i = pl.program_id(0)
    slot = i % 2

    def fetch(rt_ref, into):
        def issue(j, c):
            for u in range(ROW_DMA_UNROLL):
                r = j * ROW_DMA_UNROLL + u
                d0, d1 = _slots(pstart_ref, rt_ref, r)
                _row_copy(yb_ref, d0, rows_ref.at[into, 0], r, sems.at[into]).start(priority=0)
                _row_copy(yb_ref, d1, rows_ref.at[into, 1], r, sems.at[into]).start(priority=1)
            return c

        lax.fori_loop(0, tm // ROW_DMA_UNROLL, issue, 0)

    @pl.when(i == 0)
    def _():
        fetch(route_ref, 0)

    @pl.when(i + 1 < n_tiles)
    def _():
        fetch(nxt_ref, 1 - slot)

    def drain(j, c):
        for _ in range(2 * ROW_DMA_UNROLL):
            _row_copy(yb_ref, 0, rows_ref.at[slot, 0], 0, sems.at[slot]).wait()
        return c

    lax.fori_loop(0, tm // ROW_DMA_UNROLL, drain, 0)
    w = rw_ref[...]
    y = h2_ref[...] + (_load_rows(rows_ref.at[slot, 0]) * w[:, 0:1] + _load_rows(rows_ref.at[slot, 1]) * w[:, 1:2])

    @pl.when(i < prompt_tiles)
    def _():
        yp_ref[...] = y

    @pl.when(i >= prompt_tiles)
    def _():
        ys_ref[...] = y


def _combine(h2, rw, route, pstart, yb, tm, n_prompt):
    N, D = h2.shape
    C, W = yb.shape[1:]
    n_tiles, tp = N // tm, n_prompt // tm
    idx = lambda off: pl.BlockSpec((None, 1, 4 * tm), lambda i, ps: (jnp.minimum(i + off, n_tiles - 1), 0, 0),
                                   memory_space=pltpu.SMEM)
    grid_spec = pltpu.PrefetchScalarGridSpec(
        num_scalar_prefetch=1,
        grid=(n_tiles,),
        in_specs=[idx(0), idx(1),
                  pl.BlockSpec((tm, D), lambda i, ps: (i, 0)),
                  pl.BlockSpec((tm, ROUTER_LANES), lambda i, ps: (i, 0)),
                  pl.BlockSpec(memory_space=pl.ANY)],
        out_specs=[pl.BlockSpec((tm, D), lambda i, ps: (jnp.minimum(i, tp - 1), 0)),
                   pl.BlockSpec((tm, D), lambda i, ps: (jnp.maximum(i - tp, 0), 0))],
        scratch_shapes=[pltpu.VMEM((2, 2, tm, C, W), F32), pltpu.SemaphoreType.DMA((2,))],
    )
    return pl.pallas_call(
        functools.partial(_combine_body, tm=tm, prompt_tiles=tp, n_tiles=n_tiles),
        grid_spec=grid_spec,
        out_shape=(jax.ShapeDtypeStruct((n_prompt, D), F32), jax.ShapeDtypeStruct((N - n_prompt, D), F32)),
        compiler_params=_params("arbitrary"),
        name="combine",
    )(pstart, route, route, h2, rw, yb)


def _moe(h2, hn2, ri, rw, counts, w_gate, w_up, w_down, tm, n_prompt, bm):
    N, D = h2.shape
    cnt = counts[0, N_GROUPS:N_GROUPS + N_EXPERTS].astype(jnp.int32)
    padded = (cnt + bm - 1) // bm * bm
    pend = jnp.cumsum(padded).astype(jnp.int32)
    pstart = pend - padded
    route = ri[:, 0:4].reshape(N // tm, 1, 4 * tm)
    cap = -(-(2 * N + N_EXPERTS * (bm - 1)) // bm) * bm
    nblk = cap // bm
    starts = jnp.arange(nblk, dtype=jnp.int32)[:, None] * bm
    blk_e = jnp.minimum(jnp.sum((pend[None, :] <= starts).astype(jnp.int32), axis=1), N_EXPERTS - 1)
    nact = pend[-1:] // bm
    buf = _dispatch(hn2, route, pend, padded, pstart, cap, bm, tm)
    yb = _experts(buf, blk_e, nact, w_gate, w_up, w_down, bm)
    return _combine(h2, rw, route, pstart, yb, tm, n_prompt)


def _constants(tm):
    r = lax.broadcasted_iota(jnp.int32, (KEY_BLOCK, KEY_BLOCK), 0)
    c = lax.broadcasted_iota(jnp.int32, (KEY_BLOCK, KEY_BLOCK), 1)
    tri = jnp.where(r > c, -1.0, 0.0).astype(BF16)
    hr = lax.broadcasted_iota(jnp.int32, (SB_WIDTH, SB_WIDTH), 0) // SB_HEAD_DIM
    hc = lax.broadcasted_iota(jnp.int32, (SB_WIDTH, SB_WIDTH), 1) // SB_HEAD_DIM
    hsum = jnp.where(hr == hc, 1.0 / SB_HEAD_DIM, 0.0).astype(BF16)
    lr = lax.broadcasted_iota(jnp.int32, (tm, tm), 0)
    lc = lax.broadcasted_iota(jnp.int32, (tm, tm), 1)
    lower = jnp.where(lc < lr, 1.0, 0.0).astype(BF16)
    return tri, hsum, lower


def kernel(x_prompt, x_sample, cache_sb_k, cache_sb_v, state_pool, meta_tokens, norm_mix, w_in, q_norm, k_norm,
           pool_mix, pool_scale, w_pool_out, w_sb_out, w_o, norm_ffn, w_router_group, b_router_group,
           w_router_expert, b_router_expert, w_gate, w_up, w_down):
    assert norm_mix.shape[0] == 1, "single-layer trunk"
    B, S, D = x_prompt.shape
    BS, SS, _ = x_sample.shape
    past = cache_sb_k.shape[2]
    tm = min(512, B * S, BS * SS)
    tri, hsum, lower = _constants(tm)

    w_in0 = w_in[0]
    w_uqkv = w_in0[:, :4 * SB_WIDTH].astype(BF16)
    w_gates = w_in0[:, 4 * SB_WIDTH:].astype(BF16)
    nm = norm_mix[0][None, :]
    nf = norm_ffn[0][None, :]
    qg = jnp.tile(q_norm[0], SB_HEADS)[None, :]
    kg = jnp.tile(k_norm[0], SB_HEADS)[None, :]
    pm = pool_mix[0].astype(BF16)
    ps = pool_scale[0][None, :]
    wpo = w_pool_out[0].astype(BF16)
    wsb = w_sb_out[0].astype(BF16)
    wo = w_o[0].astype(BF16)
    pad = ROUTER_LANES - N_GROUPS - N_EXPERTS
    w_router = jnp.concatenate([w_router_group[0], w_router_expert[0], jnp.zeros((D, pad), F32)], axis=1).astype(BF16)
    b_router = jnp.concatenate([b_router_group[0], b_router_expert[0], jnp.zeros((pad,), F32)])[None, :]

    inproj = functools.partial(_inproj, norm_mix=nm, w_uqkv=w_uqkv, qg=qg, kg=kg, hsum=hsum, pool_mix=pm,
                               pool_scale=ps)

    shape5 = lambda a: a.reshape(1, a.shape[0], a.shape[1], SB_HEADS, SB_HEAD_DIM)
    flat = lambda a: a.reshape(a.shape[0] * a.shape[1], a.shape[2])

    prev_s = jnp.concatenate([jnp.zeros((BS, CARRY_ROWS - POOL_STATE, POOL_WIDTH), F32), state_pool[0]], axis=1)
    q_s, k_s, v_s, ks_f, vs_f, po_s, utail_s = inproj(x_sample, prev_s)
    kc = jnp.transpose(cache_sb_k[0], (0, 2, 3, 1)).reshape(BS, SB_WIDTH, past)
    vc = jnp.transpose(cache_sb_v[0], (0, 2, 3, 1)).reshape(BS, SB_WIDTH, past)
    sb_s = _attention(q_s, k_s, v_s, kc, vc, tri, keys_on_lanes=True)
    new_pool_sample = utail_s[None, :, CARRY_ROWS - POOL_STATE:, :]

    zeros_prev = jnp.zeros((1, CARRY_ROWS, POOL_WIDTH), F32)
    _, km_b, vm_b, km_f, vm_f, _, u_meta = inproj(meta_tokens[None], zeros_prev)

    q_b, k_b, v_b, k_f, v_f, po, utail = inproj(x_prompt, u_meta, k_lead=km_f, v_lead=vm_f)
    sb = _attention(q_b, k_b, v_b, km_b, vm_b, tri)
    new_k_prompt, new_v_prompt = shape5(k_f), shape5(v_f)
    new_pool_prompt = utail[None, :, CARRY_ROWS - POOL_STATE:, :]

    h2, hn2, ri, rw, counts = _merge_route((flat(x_prompt), flat(po), flat(sb)), (flat(x_sample), flat(po_s), flat(sb_s)),
                                           nm, w_gates, wpo, wsb, wo, nf, w_router, b_router, lower)
    y_p, y_s = _moe(h2, hn2, ri, rw, counts, w_gate[0], w_up[0], w_down[0], tm, B * S, MOE_BLOCK_ROWS)
    y_prompt, y_sample = y_p.reshape(x_prompt.shape), y_s.reshape(x_sample.shape)

    return (y_prompt, y_sample, new_k_prompt, new_v_prompt, new_pool_prompt,
            shape5(ks_f), shape5(vs_f), new_pool_sample)
```

```python
import functools

import jax
import jax.numpy as jnp
from jax import lax
from jax.experimental import pallas as pl
from jax.experimental.pallas import tpu as pltpu

F32 = jnp.float32
BF16 = jnp.bfloat16

N_META = 16
POOL_WINDOWS = (2, 4, 8, 16)
POOL_GROUP = 128
POOL_WIDTH = 512
POOL_STATE = 15
SB_HEADS = 8
SB_HEAD_DIM = 64
SB_WIDTH = 512
HEAD_PAIR_LANES = 2 * SB_HEAD_DIM
N_GROUPS = 4
EXPERTS_PER_GROUP = 8
N_EXPERTS = 32
ROUTER_LANES = 128
MOE_BLOCK_ROWS = 512
ROW_TILE = (8, 128)
ROW_DMA_UNROLL = 8
NORM_EPS = 1e-6
CARRY_ROWS = 16
KEY_BLOCK = 256
PHASE2_GROUP = 4
VMEM_LIMIT_BYTES = 56 * 1024 * 1024


def _params(*sem):
    return pltpu.CompilerParams(dimension_semantics=sem, vmem_limit_bytes=VMEM_LIMIT_BYTES)


def _rms(x, gain):
    ms = jnp.mean(x * x, axis=-1, keepdims=True)
    return x * lax.rsqrt(ms + NORM_EPS) * gain


def _inproj_body(x_ref, nm_ref, w_ref, qg_ref, kg_ref, hs_ref, pm_ref, ps_ref, uprev_ref, klead_ref, vlead_ref,
                 qb_ref, kb_ref, vb_ref, kf_ref, vf_ref, po_ref, ut_ref, carry_ref, kcarry_ref, vcarry_ref,
                 *, tm, n_tiles, lead):
    t = pl.program_id(1)

    @pl.when(t == 0)
    def _():
        carry_ref[...] = uprev_ref[...]
        if lead:
            kcarry_ref[...] = klead_ref[...]
            vcarry_ref[...] = vlead_ref[...]

    if lead:
        @pl.when(t == n_tiles)
        def _():
            kf_ref[0:lead, :] = kcarry_ref[...]
            vf_ref[0:lead, :] = vcarry_ref[...]

    @pl.when(t < n_tiles)
    def _():
        _inproj_tile(x_ref, nm_ref, w_ref, qg_ref, kg_ref, hs_ref, pm_ref, ps_ref,
                     qb_ref, kb_ref, vb_ref, kf_ref, vf_ref, po_ref, ut_ref, carry_ref, kcarry_ref, vcarry_ref,
                     tm=tm, lead=lead)


def _inproj_tile(x_ref, nm_ref, w_ref, qg_ref, kg_ref, hs_ref, pm_ref, ps_ref,
                 qb_ref, kb_ref, vb_ref, kf_ref, vf_ref, po_ref, ut_ref, carry_ref, kcarry_ref, vcarry_ref, *, tm, lead):
    hn = _rms(x_ref[...], nm_ref[...]).astype(BF16)
    z = jnp.dot(hn, w_ref[...], preferred_element_type=F32)
    u = z[:, 0:POOL_WIDTH]
    v = z[:, 3 * SB_WIDTH:4 * SB_WIDTH]

    def head_norm(a, gain):
        ms = jnp.dot((a * a).astype(BF16), hs_ref[...], preferred_element_type=F32)
        return a * lax.rsqrt(ms + NORM_EPS) * gain

    qn = head_norm(z[:, SB_WIDTH:2 * SB_WIDTH], qg_ref[...])
    kn = head_norm(z[:, 2 * SB_WIDTH:3 * SB_WIDTH], kg_ref[...])
    qb_ref[...] = (qn * (SB_HEAD_DIM ** -0.5)).astype(BF16)
    kb_ref[...] = kn.astype(BF16)
    vb_ref[...] = v.astype(BF16)
    if lead:
        kf_ref[...] = jnp.concatenate([kcarry_ref[...], kn[:tm - lead]], axis=0)
        vf_ref[...] = jnp.concatenate([vcarry_ref[...], v[:tm - lead]], axis=0)
        kcarry_ref[...] = kn[tm - lead:]
        vcarry_ref[...] = v[tm - lead:]
    else:
        kf_ref[...] = kn
        vf_ref[...] = v

    ext = jnp.concatenate([carry_ref[...], u], axis=0)
    mixed = []
    for gi, w in enumerate(POOL_WINDOWS):
        c0, c1 = gi * POOL_GROUP, (gi + 1) * POOL_GROUP
        s = ext[:, c0:c1]
        shift = 1
        while shift < w:
            s = s + pltpu.roll(s, shift, axis=0)
            shift *= 2
        pooled = s[CARRY_ROWS:, :] * (1.0 / w) - u[:, c0:c1]
        mixed.append(jnp.dot(pooled.astype(BF16), pm_ref[gi], preferred_element_type=F32))
    po_ref[...] = (jnp.concatenate(mixed, axis=1) * ps_ref[...]).astype(BF16)
    carry_ref[...] = u[tm - CARRY_ROWS:, :]
    ut_ref[...] = u[tm - CARRY_ROWS:, :]


def _inproj(x, uprev, norm_mix, w_uqkv, qg, kg, hsum, pool_mix, pool_scale, k_lead=None, v_lead=None):
    B, S, D = x.shape
    tm = min(512, S)
    n_tiles = S // tm
    lead = 0 if k_lead is None else k_lead.shape[1]
    assert S % tm == 0 and tm % CARRY_ROWS == 0 and lead % ROW_TILE[0] == 0 and lead < tm
    if k_lead is None:
        k_lead = v_lead = jnp.zeros((1, ROW_TILE[0], SB_WIDTH), F32)
    per_batch_prev = uprev.shape[0] == B and B > 1
    last = n_tiles - 1
    row = lambda b, t: (b, jnp.minimum(t, last), 0)
    late = lambda b, t: (b, t, 0)
    const2 = lambda b, t: (0, 0)
    out_shape = (
        jax.ShapeDtypeStruct((B, S, SB_WIDTH), BF16),
        jax.ShapeDtypeStruct((B, S, SB_WIDTH), BF16),
        jax.ShapeDtypeStruct((B, S, SB_WIDTH), BF16),
        jax.ShapeDtypeStruct((B, S + lead, SB_WIDTH), F32),
        jax.ShapeDtypeStruct((B, S + lead, SB_WIDTH), F32),
        jax.ShapeDtypeStruct((B, S, POOL_WIDTH), BF16),
        jax.ShapeDtypeStruct((B, CARRY_ROWS, POOL_WIDTH), F32),
    )
    blk = lambda w: pl.BlockSpec((None, tm, w), row)
    return pl.pallas_call(
        functools.partial(_inproj_body, tm=tm, n_tiles=n_tiles, lead=lead),
        grid=(B, n_tiles + (1 if lead else 0)),
        in_specs=[
            pl.BlockSpec((None, tm, D), row),
            pl.BlockSpec((1, D), const2),
            pl.BlockSpec((D, 4 * SB_WIDTH), const2),
            pl.BlockSpec((1, SB_WIDTH), const2),
            pl.BlockSpec((1, SB_WIDTH), const2),
            pl.BlockSpec((SB_WIDTH, SB_WIDTH), const2),
            pl.BlockSpec((len(POOL_WINDOWS), POOL_GROUP, POOL_GROUP), lambda b, t: (0, 0, 0)),
            pl.BlockSpec((1, POOL_WIDTH), const2),
            pl.BlockSpec((None, CARRY_ROWS, POOL_WIDTH),
                         (lambda b, t: (b, 0, 0)) if per_batch_prev else (lambda b, t: (0, 0, 0))),
            pl.BlockSpec((None,) + k_lead.shape[1:], lambda b, t: (0, 0, 0)),
            pl.BlockSpec((None,) + v_lead.shape[1:], lambda b, t: (0, 0, 0)),
        ],
        out_specs=[blk(SB_WIDTH), blk(SB_WIDTH), blk(SB_WIDTH),
                   pl.BlockSpec((None, tm, SB_WIDTH), late), pl.BlockSpec((None, tm, SB_WIDTH), late), blk(POOL_WIDTH),
                   pl.BlockSpec((None, CARRY_ROWS, POOL_WIDTH), lambda b, t: (b, 0, 0))],
        out_shape=out_shape,
        scratch_shapes=[pltpu.VMEM((CARRY_ROWS, POOL_WIDTH), F32),
                        pltpu.VMEM(k_lead.shape[1:], F32), pltpu.VMEM(v_lead.shape[1:], F32)],
        compiler_params=_params("arbitrary", "arbitrary"),
        name="inproj",
    )(x, norm_mix, w_uqkv, qg, kg, hsum, pool_mix, pool_scale, uprev, k_lead, v_lead)


def _log_gates(z):
    sign = jnp.uint32(0x80000000)
    neg_abs = lax.bitcast_convert_type(lax.bitcast_convert_type(z, jnp.uint32) | sign, F32)
    sp = jnp.maximum(z, 0.0) + jnp.log(1.0 + jnp.exp(neg_abs))
    return sp, z - sp


def _dot_nt(a, b):
    return lax.dot_general(a, b, (((1,), (1,)), ((), ())), preferred_element_type=F32)


def _attn_body(vq_ref, vk_ref, q_ref, k_ref, v_ref, kp_ref, vp_ref, tri_ref, o_ref,
               z_scr, e_scr, acc_scr, run_scr, *, L, P, tq, n_visits, full_from_prefix, keys_on_lanes):
    nq = L // tq
    lane = lax.broadcasted_iota(jnp.int32, (1, HEAD_PAIR_LANES), 1)
    heads = (lane < SB_HEAD_DIM, lane >= SB_HEAD_DIM)
    ksrc, vsrc = (kp_ref, vp_ref) if full_from_prefix else (k_ref, v_ref)

    def stacked_q(qi):
        qp = q_ref[pl.ds(pl.multiple_of(qi * tq, tq), tq), :]
        zero = jnp.zeros((), BF16)
        return jnp.concatenate([jnp.where(heads[0], qp, zero), jnp.where(heads[1], qp, zero)], axis=0)

    def key_rows(s):
        return pl.ds(pl.multiple_of(vk_ref[s] * KEY_BLOCK, KEY_BLOCK), KEY_BLOCK)

    def stage_a(s):
        if keys_on_lanes:
            z_scr[...] = jnp.dot(stacked_q(vq_ref[s]), ksrc[:, key_rows(s)].astype(BF16), preferred_element_type=F32)
        else:
            z_scr[...] = _dot_nt(stacked_q(vq_ref[s]), ksrc[key_rows(s), :].astype(BF16))

    def stage_b(s):
        qi = vq_ref[s]
        sp, ls = _log_gates(z_scr[...])
        later = jnp.dot(sp.astype(BF16), tri_ref[...], preferred_element_type=F32)
        run = run_scr[qi]
        e_scr[...] = ls + (later + jnp.concatenate([run, run], axis=1))
        run_scr[qi] = run - jnp.sum(sp, axis=1, keepdims=True)

    def stage_c(s):
        qi = vq_ref[s]
        a = jnp.exp(e_scr[...]).astype(BF16)
        if keys_on_lanes:
            acc_scr[qi] = acc_scr[qi] + _dot_nt(a, vsrc[:, key_rows(s)].astype(BF16))
        else:
            acc_scr[qi] = acc_scr[qi] + jnp.dot(a, vsrc[key_rows(s), :].astype(BF16), preferred_element_type=F32)

    if n_visits:
        acc_scr[...] = jnp.zeros_like(acc_scr)
        run_scr[...] = jnp.zeros_like(run_scr)
        stage_a(0)
        if n_visits >= 2:
            stage_b(0)
            stage_a(1)

        def steady(s, c):
            stage_c(s - 2)
            stage_b(s - 1)
            stage_a(s)
            return c

        lax.fori_loop(2, n_visits, steady, 0)
        if n_visits >= 2:
            stage_c(n_visits - 2)
        stage_b(n_visits - 1)
        stage_c(n_visits - 1)

    rows = lax.broadcasted_iota(jnp.int32, (2 * tq, tq), 0)
    cols = lax.broadcasted_iota(jnp.int32, (2 * tq, tq), 1)
    visible = (cols < rows) & (rows < tq) | (cols < rows - tq)

    def q_block(qi):
        qrows = pl.ds(pl.multiple_of(qi * tq, tq), tq)
        q2 = stacked_q(qi)
        sp, ls = _log_gates(_dot_nt(q2, k_ref[qrows, :]))
        sp = jnp.where(visible, sp, 0.0)
        later = jnp.dot(sp.astype(BF16), tri_ref[0:tq, 0:tq], preferred_element_type=F32)
        a = jnp.where(visible, jnp.exp(ls + later), 0.0)
        acc = jnp.dot(a.astype(BF16), v_ref[qrows, :], preferred_element_type=F32)
        run = -jnp.sum(sp, axis=1, keepdims=True)
        if n_visits:
            acc = acc + jnp.exp(run) * acc_scr[qi]
            run = run + run_scr[qi][:, 0:1]
        if not full_from_prefix:
            sp, ls = _log_gates(_dot_nt(q2, kp_ref[...].astype(BF16)))
            later = jnp.dot(sp.astype(BF16), tri_ref[0:P, 0:P], preferred_element_type=F32)
            a = jnp.exp(ls + (later + run))
            acc = acc + jnp.dot(a.astype(BF16), vp_ref[...].astype(BF16), preferred_element_type=F32)
        o_ref[qrows, :] = jnp.where(heads[0], acc[:tq], acc[tq:]).astype(BF16)

    def q_block_group(j, carry):
        for u in range(PHASE2_GROUP):
            q_block(PHASE2_GROUP * j + u)
        return carry

    lax.fori_loop(0, nq // PHASE2_GROUP, q_block_group, 0)
    for qi in range(nq - nq % PHASE2_GROUP, nq):
        q_block(qi)


def _attention(q, k, v, kp, vp, tri, keys_on_lanes=False):
    B, L, _ = q.shape
    P = kp.shape[2] if keys_on_lanes else kp.shape[1]
    tq = min(KEY_BLOCK, L)
    nq = L // tq
    assert L % tq == 0 and tq % 16 == 0 and P % 16 == 0
    full_from_prefix = P > KEY_BLOCK
    if full_from_prefix:
        assert nq == 1 and P % KEY_BLOCK == 0
        visits = [(qi, kb) for qi in range(nq) for kb in reversed(range(P // KEY_BLOCK))]
    else:
        assert tq == KEY_BLOCK or nq == 1
        visits = [(qi, kb) for qi in range(nq) for kb in reversed(range(qi))]
    n_visits = len(visits)
    vq = jnp.asarray([qi for qi, _ in visits] or [0], jnp.int32)
    vk = jnp.asarray([kb for _, kb in visits] or [0], jnp.int32)
    per_batch_prefix = kp.shape[0] == B and B > 1
    seq = pl.BlockSpec((None, L, HEAD_PAIR_LANES), lambda b, hp, *_: (b, 0, hp))
    if keys_on_lanes:
        assert full_from_prefix and per_batch_prefix
        pre = pl.BlockSpec((None, HEAD_PAIR_LANES, P), lambda b, hp, *_: (b, hp, 0))
    else:
        pre = pl.BlockSpec((None, P, HEAD_PAIR_LANES),
                           (lambda b, hp, *_: (b, 0, hp)) if per_batch_prefix else (lambda b, hp, *_: (0, 0, hp)))
    grid_spec = pltpu.PrefetchScalarGridSpec(
        num_scalar_prefetch=2,
        grid=(B, SB_WIDTH // HEAD_PAIR_LANES),
        in_specs=[seq, seq, seq, pre, pre, pl.BlockSpec((KEY_BLOCK, KEY_BLOCK), lambda b, hp, *_: (0, 0))],
        out_specs=seq,
        scratch_shapes=[pltpu.VMEM((2 * tq, KEY_BLOCK), F32), pltpu.VMEM((2 * tq, KEY_BLOCK), F32),
                        pltpu.VMEM((nq, 2 * tq, HEAD_PAIR_LANES), F32),
                        pltpu.VMEM((nq, 2 * tq, HEAD_PAIR_LANES), F32)],
    )
    return pl.pallas_call(
        functools.partial(_attn_body, L=L, P=P, tq=tq, n_visits=n_visits, full_from_prefix=full_from_prefix,
                          keys_on_lanes=keys_on_lanes),
        grid_spec=grid_spec,
        out_shape=jax.ShapeDtypeStruct((B, L, SB_WIDTH), BF16),
        compiler_params=_params("arbitrary", "arbitrary"),
        name="stickbreak",
    )(vq, vk, q, k, v, kp, vp, tri)


def _transpose8(v):
    sub = lax.broadcasted_iota(jnp.int32, (1, ROW_TILE[0], 1), 1)
    for k in (4, 2, 1):
        low = (sub & k) == 0
        nxt = list(v)
        for i in range(ROW_TILE[0]):
            if i & k == 0:
                j = i | k
                nxt[i] = jnp.where(low, v[i], pltpu.roll(v[j], k, axis=1))
                nxt[j] = jnp.where(low, pltpu.roll(v[i], ROW_TILE[0] - k, axis=1), v[j])
        v = nxt
    return v


def _store_rows(ref, val):
    s, w = ROW_TILE
    g = val.shape[0] // s
    v = _transpose8([val[:, c * w:(c + 1) * w].reshape(g, s, w) for c in range(s)])
    for r in range(s):
        ref[pl.ds(r, g, stride=s), :, :] = v[r]


def _load_rows(ref):
    s, w = ROW_TILE
    g = ref.shape[0] // s
    v = _transpose8([ref[pl.ds(r, g, stride=s), :, :] for r in range(s)])
    return jnp.concatenate([x.reshape(g * s, w) for x in v], axis=1)


def _merge_body(xp_ref, pop_ref, sbp_ref, xs_ref, pos_ref, sbs_ref,
                nm_ref, wg_ref, wpo_ref, wsb_ref, wo_ref, nf_ref, wr_ref, br_ref, low_ref,
                h2_ref, hn2_ref, ri_ref, rw_ref, cnt_ref, carry_ref, *, prompt_tiles):
    @pl.when(pl.program_id(0) == 0)
    def _():
        carry_ref[...] = jnp.zeros_like(carry_ref)

    from_prompt = pl.program_id(0) < prompt_tiles
    x = jnp.where(from_prompt, xp_ref[...], xs_ref[...])
    po = jnp.where(from_prompt, pop_ref[...], pos_ref[...])
    sb = jnp.where(from_prompt, sbp_ref[...], sbs_ref[...])
    D = x.shape[1]
    hn = _rms(x, nm_ref[...]).astype(BF16)
    gates = jnp.dot(hn, wg_ref[...], preferred_element_type=F32)
    pool_proj = jnp.dot(po, wpo_ref[...], preferred_element_type=F32)
    sb_proj = jnp.dot(sb, wsb_ref[...], preferred_element_type=F32)
    merged = jax.nn.sigmoid(gates[:, :D]) * pool_proj + jax.nn.sigmoid(gates[:, D:]) * sb_proj
    h2 = x + jnp.dot(merged.astype(BF16), wo_ref[...], preferred_element_type=F32)
    h2_ref[...] = h2
    hn2 = _rms(h2, nf_ref[...])
    _store_rows(hn2_ref, hn2)

    logits = jnp.dot(hn2.astype(BF16), wr_ref[...], preferred_element_type=F32) + br_ref[...]
    lane = lax.broadcasted_iota(jnp.int32, logits.shape, 1)
    lanef = lane.astype(F32)
    neg = jnp.float32(-jnp.inf)
    big = jnp.float32(2 * ROUTER_LANES)

    def first_argmax(vals):
        m = jnp.max(vals, axis=1, keepdims=True)
        return m, jnp.min(jnp.where(vals == m, lanef, big), axis=1, keepdims=True)

    is_group = lane < N_GROUPS
    gmax, gidx = first_argmax(jnp.where(is_group, logits, neg))
    p_group = 1.0 / jnp.sum(jnp.where(is_group, jnp.exp(logits - gmax), 0.0), axis=1, keepdims=True)
    first = N_GROUPS + EXPERTS_PER_GROUP * gidx
    in_group = (lanef >= first) & (lanef < first + EXPERTS_PER_GROUP)
    cand = jnp.where(in_group, logits, neg)
    m1, i1 = first_argmax(cand)
    m2, i2 = first_argmax(jnp.where(lanef == i1, neg, cand))
    t = jnp.exp(m2 - m1)
    w1 = p_group / (1.0 + t)
    w2 = p_group * t / (1.0 + t)

    hit1 = lanef == i1
    hit2 = lanef == i2
    onehot = jnp.where(hit1 | hit2, 1.0, 0.0)
    before = jnp.dot(low_ref[...], onehot.astype(BF16), preferred_element_type=F32) + carry_ref[...]
    r1 = jnp.sum(jnp.where(hit1, before, 0.0), axis=1, keepdims=True)
    r2 = jnp.sum(jnp.where(hit2, before, 0.0), axis=1, keepdims=True)
    carry_ref[...] = carry_ref[...] + jnp.sum(onehot, axis=0, keepdims=True)
    cnt_ref[...] = carry_ref[...]

    slab = jnp.where(lane == 0, i1 - N_GROUPS,
                     jnp.where(lane == 1, i2 - N_GROUPS, jnp.where(lane == 2, r1, jnp.where(lane == 3, r2, 0.0))))
    ri_ref[...] = slab.astype(jnp.int32)
    rw_ref[...] = jnp.where(lane == 0, w1, jnp.where(lane == 1, w2, 0.0))


def _merge_route(prompt, sample, norm_mix, w_gates, w_pool_out, w_sb_out, w_o, norm_ffn, w_router, b_router, lower):
    tm = lower.shape[0]
    (xp, pop, sbp), (xs, pos, sbs) = prompt, sample
    D = xp.shape[1]
    assert xp.shape[0] % tm == 0 and xs.shape[0] % tm == 0 and D == ROW_TILE[0] * ROW_TILE[1]
    tp, ts = xp.shape[0] // tm, xs.shape[0] // tm
    N = xp.shape[0] + xs.shape[0]
    from_p = lambda i: (jnp.minimum(i, tp - 1), 0)
    from_s = lambda i: (jnp.maximum(i - tp, 0), 0)
    row = lambda i: (i, 0)
    const = lambda i: (0, 0)
    full = lambda a: pl.BlockSpec(a.shape, const)
    group = lambda m: [pl.BlockSpec((tm, D), m), pl.BlockSpec((tm, POOL_WIDTH), m), pl.BlockSpec((tm, SB_WIDTH), m)]
    return pl.pallas_call(
        functools.partial(_merge_body, prompt_tiles=tp),
        grid=(tp + ts,),
        in_specs=group(from_p) + group(from_s) + [
            full(norm_mix), full(w_gates), full(w_pool_out), full(w_sb_out), full(w_o), full(norm_ffn),
            full(w_router), full(b_router), full(lower)],
        out_specs=[pl.BlockSpec((tm, D), row), pl.BlockSpec((tm,) + ROW_TILE, lambda i: (i, 0, 0)),
                   pl.BlockSpec((tm, ROUTER_LANES), row), pl.BlockSpec((tm, ROUTER_LANES), row),
                   pl.BlockSpec((1, ROUTER_LANES), const)],
        out_shape=(jax.ShapeDtypeStruct((N, D), F32), jax.ShapeDtypeStruct((N,) + ROW_TILE, F32),
                   jax.ShapeDtypeStruct((N, ROUTER_LANES), jnp.int32), jax.ShapeDtypeStruct((N, ROUTER_LANES), F32),
                   jax.ShapeDtypeStruct((1, ROUTER_LANES), F32)),
        scratch_shapes=[pltpu.VMEM((1, ROUTER_LANES), F32)],
        compiler_params=_params("arbitrary"),
        name="merge_route",
    )(xp, pop, sbp, xs, pos, sbs, norm_mix, w_gates, w_pool_out, w_sb_out, w_o, norm_ffn, w_router, b_router, lower)


def _row_copy(src_ref, src_row, dst_ref, dst_row, sem):
    return pltpu.make_async_copy(src_ref.at[src_row], dst_ref.at[dst_row], sem)


def _slots(pstart_ref, route_ref, r):
    return (pstart_ref[route_ref[0, 4 * r]] + route_ref[0, 4 * r + 2],
            pstart_ref[route_ref[0, 4 * r + 1]] + route_ref[0, 4 * r + 3])


def _dispatch_body(pend_ref, padded_ref, pstart_ref, route_ref, x_ref, buf_ref, zero_ref, sem, zsem, *, tm, bm):
    @pl.when(pl.program_id(0) == 0)
    def _():
        zero_ref[...] = jnp.zeros_like(zero_ref)
        for e in range(N_EXPERTS):
            @pl.when(padded_ref[e] > 0)
            def _():
                cp = pltpu.make_async_copy(zero_ref, buf_ref.at[pl.ds(pend_ref[e] - bm, bm)], zsem)
                cp.start()
                cp.wait()

        def zero_unused_block(bi, c):
            cp = pltpu.make_async_copy(zero_ref, buf_ref.at[pl.ds(bi * bm, bm)], zsem)
            cp.start()
            cp.wait()
            return c

        lax.fori_loop(lax.div(pend_ref[N_EXPERTS - 1], bm), buf_ref.shape[0] // bm, zero_unused_block, 0)

    def issue(j, c):
        for u in range(ROW_DMA_UNROLL):
            r = j * ROW_DMA_UNROLL + u
            d0, d1 = _slots(pstart_ref, route_ref, r)
            _row_copy(x_ref, r, buf_ref, d0, sem).start(priority=0)
            _row_copy(x_ref, r, buf_ref, d1, sem).start(priority=1)
        return c

    lax.fori_loop(0, tm // ROW_DMA_UNROLL, issue, 0)

    def drain(j, c):
        for _ in range(2 * ROW_DMA_UNROLL):
            _row_copy(x_ref, 0, buf_ref, 0, sem).wait()
        return c

    lax.fori_loop(0, tm // ROW_DMA_UNROLL, drain, 0)


def _dispatch(hn2, route, pend, padded, pstart, cap, bm, tm):
    N, C, W = hn2.shape
    grid_spec = pltpu.PrefetchScalarGridSpec(
        num_scalar_prefetch=3,
        grid=(N // tm,),
        in_specs=[pl.BlockSpec((None, 1, 4 * tm), lambda i, *_: (i, 0, 0), memory_space=pltpu.SMEM),
                  pl.BlockSpec((tm, C, W), lambda i, *_: (i, 0, 0))],
        out_specs=pl.BlockSpec(memory_space=pl.ANY),
        scratch_shapes=[pltpu.VMEM((bm, C, W), F32), pltpu.SemaphoreType.DMA, pltpu.SemaphoreType.DMA],
    )
    return pl.pallas_call(
        functools.partial(_dispatch_body, tm=tm, bm=bm),
        grid_spec=grid_spec,
        out_shape=jax.ShapeDtypeStruct((cap, C, W), F32),
        compiler_params=_params("arbitrary"),
        name="dispatch",
    )(pend, padded, pstart, route, hn2)


def _expert_body(blk_e_ref, nact_ref, x_ref, wg_ref, wu_ref, wd_ref, y_ref, wg_bf, wu_bf, wd_bf):
    i = pl.program_id(0)

    @pl.when(i < nact_ref[0])
    def _():
        @pl.when((i == 0) | (blk_e_ref[i] != blk_e_ref[jnp.maximum(i - 1, 0)]))
        def _():
            wg_bf[...] = wg_ref[...].astype(BF16)
            wu_bf[...] = wu_ref[...].astype(BF16)
            wd_bf[...] = wd_ref[...].astype(BF16)

        x = _load_rows(x_ref).astype(BF16)
        gate = jnp.dot(x, wg_bf[...], preferred_element_type=F32)
        up = jnp.dot(x, wu_bf[...], preferred_element_type=F32)
        hidden = (jax.nn.silu(gate) * up).astype(BF16)
        _store_rows(y_ref, jnp.dot(hidden, wd_bf[...], preferred_element_type=F32))

    @pl.when(pl.program_id(0) >= nact_ref[0])
    def _():
        y_ref[...] = jnp.zeros_like(y_ref)


def _experts(buf, blk_e, nact, w_gate, w_up, w_down, bm):
    cap, C, W = buf.shape
    D, DE = w_gate.shape[1], w_gate.shape[2]
    rows = lambda i, be, na: (jnp.minimum(i, na[0] - 1), 0, 0)
    grid_spec = pltpu.PrefetchScalarGridSpec(
        num_scalar_prefetch=2,
        grid=(cap // bm,),
        in_specs=[pl.BlockSpec((bm, C, W), rows),
                  pl.BlockSpec((None, D, DE), lambda i, be, na: (be[i], 0, 0)),
                  pl.BlockSpec((None, D, DE), lambda i, be, na: (be[i], 0, 0)),
                  pl.BlockSpec((None, DE, D), lambda i, be, na: (be[i], 0, 0))],
        out_specs=pl.BlockSpec((bm, C, W), lambda i, be, na: (i, 0, 0)),
        scratch_shapes=[pltpu.VMEM((D, DE), BF16), pltpu.VMEM((D, DE), BF16), pltpu.VMEM((DE, D), BF16)],
    )
    return pl.pallas_call(
        _expert_body,
        grid_spec=grid_spec,
        out_shape=jax.ShapeDtypeStruct((cap, C, W), F32),
        compiler_params=_params("arbitrary"),
        name="experts",
    )(blk_e, nact, buf, w_gate, w_up, w_down)


def _combine_body(pstart_ref, route_ref, nxt_ref, h2_ref, rw_ref, yb_ref, yp_ref, ys_ref, rows_ref, sems,
                  *, tm, prompt_tiles, n_tiles):
    i = pl.program_id(0)
    slot = i % 2

    def fetch(rt_ref, into):
        def issue(j, c):
            for u in range(ROW_DMA_UNROLL):
                r = j * ROW_DMA_UNROLL + u
                d0, d1 = _slots(pstart_ref, rt_ref, r)
                _row_copy(yb_ref, d0, rows_ref.at[into, 0], r, sems.at[into]).start(priority=0)
                _row_copy(yb_ref, d1, rows_ref.at[into, 1], r, sems.at[into]).start(priority=1)
            return c

        lax.fori_loop(0, tm // ROW_DMA_UNROLL, issue, 0)

    @pl.when(i == 0)
    def _():
        fetch(route_ref, 0)

    @pl.when(i + 1 < n_tiles)
    def _():
        fetch(nxt_ref, 1 - slot)

    def drain(j, c):
        for _ in range(2 * ROW_DMA_UNROLL):
            _row_copy(yb_ref, 0, rows_ref.at[slot, 0], 0, sems.at[slot]).wait()
        return c

    lax.fori_loop(0, tm // ROW_DMA_UNROLL, drain, 0)
    w = rw_ref[...]
    y = h2_ref[...] + (_load_rows(rows_ref.at[slot, 0]) * w[:, 0:1] + _load_rows(rows_ref.at[slot, 1]) * w[:, 1:2])

    @pl.when(i < prompt_tiles)
    def _():
        yp_ref[...] = y

    @pl.when(i >= prompt_tiles)
    def _():
        ys_ref[...] = y


def _combine(h2, rw, route, pstart, yb, tm, n_prompt):
    N, D = h2.shape
    C, W = yb.shape[1:]
    n_tiles, tp = N // tm, n_prompt // tm
    idx = lambda off: pl.BlockSpec((None, 1, 4 * tm), lambda i, ps: (jnp.minimum(i + off, n_tiles - 1), 0, 0),
                                   memory_space=pltpu.SMEM)
    grid_spec = pltpu.PrefetchScalarGridSpec(
        num_scalar_prefetch=1,
        grid=(n_tiles,),
        in_specs=[idx(0), idx(1),
                  pl.BlockSpec((tm, D), lambda i, ps: (i, 0)),
                  pl.BlockSpec((tm, ROUTER_LANES), lambda i, ps: (i, 0)),
                  pl.BlockSpec(memory_space=pl.ANY)],
        out_specs=[pl.BlockSpec((tm, D), lambda i, ps: (jnp.minimum(i, tp - 1), 0)),
                   pl.BlockSpec((tm, D), lambda i, ps: (jnp.maximum(i - tp, 0), 0))],
        scratch_shapes=[pltpu.VMEM((2, 2, tm, C, W), F32), pltpu.SemaphoreType.DMA((2,))],
    )
    return pl.pallas_call(
        functools.partial(_combine_body, tm=tm, prompt_tiles=tp, n_tiles=n_tiles),
        grid_spec=grid_spec,
        out_shape=(jax.ShapeDtypeStruct((n_prompt, D), F32), jax.ShapeDtypeStruct((N - n_prompt, D), F32)),
        compiler_params=_params("arbitrary"),
        name="combine",
    )(pstart, route, route, h2, rw, yb)


def _moe(h2, hn2, ri, rw, counts, w_gate, w_up, w_down, tm, n_prompt, bm):
    N, D = h2.shape
    cnt = counts[0, N_GROUPS:N_GROUPS + N_EXPERTS].astype(jnp.int32)
    padded = (cnt + bm - 1) // bm * bm
    pend = jnp.cumsum(padded).astype(jnp.int32)
    pstart = pend - padded
    route = ri[:, 0:4].reshape(N // tm, 1, 4 * tm)
    cap = -(-(2 * N + N_EXPERTS * (bm - 1)) // bm) * bm
    nblk = cap // bm
    starts = jnp.arange(nblk, dtype=jnp.int32)[:, None] * bm
    blk_e = jnp.minimum(jnp.sum((pend[None, :] <= starts).astype(jnp.int32), axis=1), N_EXPERTS - 1)
    nact = pend[-1:] // bm
    buf = _dispatch(hn2, route, pend, padded, pstart, cap, bm, tm)
    yb = _experts(buf, blk_e, nact, w_gate, w_up, w_down, bm)
    return _combine(h2, rw, route, pstart, yb, tm, n_prompt)


def _constants(tm):
    r = lax.broadcasted_iota(jnp.int32, (KEY_BLOCK, KEY_BLOCK), 0)
    c = lax.broadcasted_iota(jnp.int32, (KEY_BLOCK, KEY_BLOCK), 1)
    tri = jnp.where(r > c, -1.0, 0.0).astype(BF16)
    hr = lax.broadcasted_iota(jnp.int32, (SB_WIDTH, SB_WIDTH), 0) // SB_HEAD_DIM
    hc = lax.broadcasted_iota(jnp.int32, (SB_WIDTH, SB_WIDTH), 1) // SB_HEAD_DIM
    hsum = jnp.where(hr == hc, 1.0 / SB_HEAD_DIM, 0.0).astype(BF16)
    lr = lax.broadcasted_iota(jnp.int32, (tm, tm), 0)
    lc = lax.broadcasted_iota(jnp.int32, (tm, tm), 1)
    lower = jnp.where(lc < lr, 1.0, 0.0).astype(BF16)
    return tri, hsum, lower


def kernel(x_prompt, x_sample, cache_sb_k, cache_sb_v, state_pool, meta_tokens, norm_mix, w_in, q_norm, k_norm,
           pool_mix, pool_scale, w_pool_out, w_sb_out, w_o, norm_ffn, w_router_group, b_router_group,
           w_router_expert, b_router_expert, w_gate, w_up, w_down):
    assert norm_mix.shape[0] == 1, "single-layer trunk"
    B, S, D = x_prompt.shape
    BS, SS, _ = x_sample.shape
    past = cache_sb_k.shape[2]
    tm = min(512, B * S, BS * SS)
    tri, hsum, lower = _constants(tm)

    w_in0 = w_in[0]
    w_uqkv = w_in0[:, :4 * SB_WIDTH].astype(BF16)
    w_gates = w_in0[:, 4 * SB_WIDTH:].astype(BF16)
    nm = norm_mix[0][None, :]
    nf = norm_ffn[0][None, :]
    qg = jnp.tile(q_norm[0], SB_HEADS)[None, :]
    kg = jnp.tile(k_norm[0], SB_HEADS)[None, :]
    pm = pool_mix[0].astype(BF16)
    ps = pool_scale[0][None, :]
    wpo = w_pool_out[0].astype(BF16)
    wsb = w_sb_out[0].astype(BF16)
    wo = w_o[0].astype(BF16)
    pad = ROUTER_LANES - N_GROUPS - N_EXPERTS
    w_router = jnp.concatenate([w_router_group[0], w_router_expert[0], jnp.zeros((D, pad), F32)], axis=1).astype(BF16)
    b_router = jnp.concatenate([b_router_group[0], b_router_expert[0], jnp.zeros((pad,), F32)])[None, :]

    inproj = functools.partial(_inproj, norm_mix=nm, w_uqkv=w_uqkv, qg=qg, kg=kg, hsum=hsum, pool_mix=pm,
                               pool_scale=ps)

    shape5 = lambda a: a.reshape(1, a.shape[0], a.shape[1], SB_HEADS, SB_HEAD_DIM)
    flat = lambda a: a.reshape(a.shape[0] * a.shape[1], a.shape[2])

    prev_s = jnp.concatenate([jnp.zeros((BS, CARRY_ROWS - POOL_STATE, POOL_WIDTH), F32), state_pool[0]], axis=1)
    q_s, k_s, v_s, ks_f, vs_f, po_s, utail_s = inproj(x_sample, prev_s)
    kc = jnp.transpose(cache_sb_k[0], (0, 2, 3, 1)).reshape(BS, SB_WIDTH, past)
    vc = jnp.transpose(cache_sb_v[0], (0, 2, 3, 1)).reshape(BS, SB_WIDTH, past)
    sb_s = _attention(q_s, k_s, v_s, kc, vc, tri, keys_on_lanes=True)
    new_pool_sample = utail_s[None, :, CARRY_ROWS - POOL_STATE:, :]

    zeros_prev = jnp.zeros((1, CARRY_ROWS, POOL_WIDTH), F32)
    _, km_b, vm_b, km_f, vm_f, _, u_meta = inproj(meta_tokens[None], zeros_prev)

    q_b, k_b, v_b, k_f, v_f, po, utail = inproj(x_prompt, u_meta, k_lead=km_f, v_lead=vm_f)
    sb = _attention(q_b, k_b, v_b, km_b, vm_b, tri)
    new_k_prompt, new_v_prompt = shape5(k_f), shape5(v_f)
    new_pool_prompt = utail[None, :, CARRY_ROWS - POOL_STATE:, :]

    h2, hn2, ri, rw, counts = _merge_route((flat(x_prompt), flat(po), flat(sb)), (flat(x_sample), flat(po_s), flat(sb_s)),
                                           nm, w_gates, wpo, wsb, wo, nf, w_router, b_router, lower)
    y_p, y_s = _moe(h2, hn2, ri, rw, counts, w_gate[0], w_up[0], w_down[0], tm, B * S, MOE_BLOCK_ROWS)
    y_prompt, y_sample = y_p.reshape(x_prompt.shape), y_s.reshape(x_sample.shape)

    return (y_prompt, y_sample, new_k_prompt, new_v_prompt, new_pool_prompt,
            shape5(ks_f), shape5(vs_f), new_pool_sample)
```

```python
import functools

import jax
import jax.numpy as jnp
from jax import lax
from jax.experimental import pallas as pl
from jax.experimental.pallas import tpu as pltpu

F32 = jnp.float32
BF16 = jnp.bfloat16

N_META = 16
POOL_WINDOWS = (2, 4, 8, 16)
POOL_GROUP = 128
POOL_WIDTH = 512
POOL_STATE = 15
SB_HEADS = 8
SB_HEAD_DIM = 64
SB_WIDTH = 512
HEAD_PAIR_LANES = 2 * SB_HEAD_DIM
N_GROUPS = 4
EXPERTS_PER_GROUP = 8
N_EXPERTS = 32
ROUTER_LANES = 128
MOE_BLOCK_ROWS = 512
ROW_TILE = (8, 128)
ROW_DMA_UNROLL = 8
NORM_EPS = 1e-6
CARRY_ROWS = 16
KEY_BLOCK = 256
PHASE2_GROUP = 4
VMEM_LIMIT_BYTES = 56 * 1024 * 1024


def _params(*sem):
    return pltpu.CompilerParams(dimension_semantics=sem, vmem_limit_bytes=VMEM_LIMIT_BYTES)


def _rms(x, gain):
    ms = jnp.mean(x * x, axis=-1, keepdims=True)
    return x * lax.rsqrt(ms + NORM_EPS) * gain


def _inproj_body(x_ref, nm_ref, w_ref, qg_ref, kg_ref, hs_ref, pm_ref, ps_ref, uprev_ref, klead_ref, vlead_ref,
                 qb_ref, kb_ref, vb_ref, kf_ref, vf_ref, po_ref, ut_ref, carry_ref, kcarry_ref, vcarry_ref,
                 *, tm, n_tiles, lead):
    t = pl.program_id(1)

    @pl.when(t == 0)
    def _():
        carry_ref[...] = uprev_ref[...]
        if lead:
            kcarry_ref[...] = klead_ref[...]
            vcarry_ref[...] = vlead_ref[...]

    if lead:
        @pl.when(t == n_tiles)
        def _():
            kf_ref[0:lead, :] = kcarry_ref[...]
            vf_ref[0:lead, :] = vcarry_ref[...]

    @pl.when(t < n_tiles)
    def _():
        _inproj_tile(x_ref, nm_ref, w_ref, qg_ref, kg_ref, hs_ref, pm_ref, ps_ref,
                     qb_ref, kb_ref, vb_ref, kf_ref, vf_ref, po_ref, ut_ref, carry_ref, kcarry_ref, vcarry_ref,
                     tm=tm, lead=lead)


def _inproj_tile(x_ref, nm_ref, w_ref, qg_ref, kg_ref, hs_ref, pm_ref, ps_ref,
                 qb_ref, kb_ref, vb_ref, kf_ref, vf_ref, po_ref, ut_ref, carry_ref, kcarry_ref, vcarry_ref, *, tm, lead):
    hn = _rms(x_ref[...], nm_ref[...]).astype(BF16)
    z = jnp.dot(hn, w_ref[...], preferred_element_type=F32)
    u = z[:, 0:POOL_WIDTH]
    v = z[:, 3 * SB_WIDTH:4 * SB_WIDTH]

    def head_norm(a, gain):
        ms = jnp.dot((a * a).astype(BF16), hs_ref[...], preferred_element_type=F32)
        return a * lax.rsqrt(ms + NORM_EPS) * gain

    qn = head_norm(z[:, SB_WIDTH:2 * SB_WIDTH], qg_ref[...])
    kn = head_norm(z[:, 2 * SB_WIDTH:3 * SB_WIDTH], kg_ref[...])
    qb_ref[...] = (qn * (SB_HEAD_DIM ** -0.5)).astype(BF16)
    kb_ref[...] = kn.astype(BF16)
    vb_ref[...] = v.astype(BF16)
    if lead:
        kf_ref[...] = jnp.concatenate([kcarry_ref[...], kn[:tm - lead]], axis=0)
        vf_ref[...] = jnp.concatenate([vcarry_ref[...], v[:tm - lead]], axis=0)
        kcarry_ref[...] = kn[tm - lead:]
        vcarry_ref[...] = v[tm - lead:]
    else:
        kf_ref[...] = kn
        vf_ref[...] = v

    ext = jnp.concatenate([carry_ref[...], u], axis=0)
    mixed = []
    for gi, w in enumerate(POOL_WINDOWS):
        c0, c1 = gi * POOL_GROUP, (gi + 1) * POOL_GROUP
        s = ext[:, c0:c1]
        shift = 1
        while shift < w:
            s = s + pltpu.roll(s, shift, axis=0)
            shift *= 2
        pooled = s[CARRY_ROWS:, :] * (1.0 / w) - u[:, c0:c1]
        mixed.append(jnp.dot(pooled.astype(BF16), pm_ref[gi], preferred_element_type=F32))
    po_ref[...] = (jnp.concatenate(mixed, axis=1) * ps_ref[...]).astype(BF16)
    carry_ref[...] = u[tm - CARRY_ROWS:, :]
    ut_ref[...] = u[tm - CARRY_ROWS:, :]


def _inproj(x, uprev, norm_mix, w_uqkv, qg, kg, hsum, pool_mix, pool_scale, k_lead=None, v_lead=None):
    B, S, D = x.shape
    tm = min(512, S)
    n_tiles = S // tm
    lead = 0 if k_lead is None else k_lead.shape[1]
    assert S % tm == 0 and tm % CARRY_ROWS == 0 and lead % ROW_TILE[0] == 0 and lead < tm
    if k_lead is None:
        k_lead = v_lead = jnp.zeros((1, ROW_TILE[0], SB_WIDTH), F32)
    per_batch_prev = uprev.shape[0] == B and B > 1
    last = n_tiles - 1
    row = lambda b, t: (b, jnp.minimum(t, last), 0)
    late = lambda b, t: (b, t, 0)
    const2 = lambda b, t: (0, 0)
    out_shape = (
        jax.ShapeDtypeStruct((B, S, SB_WIDTH), BF16),
        jax.ShapeDtypeStruct((B, S, SB_WIDTH), BF16),
        jax.ShapeDtypeStruct((B, S, SB_WIDTH), BF16),
        jax.ShapeDtypeStruct((B, S + lead, SB_WIDTH), F32),
        jax.ShapeDtypeStruct((B, S + lead, SB_WIDTH), F32),
        jax.ShapeDtypeStruct((B, S, POOL_WIDTH), BF16),
        jax.ShapeDtypeStruct((B, CARRY_ROWS, POOL_WIDTH), F32),
    )
    blk = lambda w: pl.BlockSpec((None, tm, w), row)
    return pl.pallas_call(
        functools.partial(_inproj_body, tm=tm, n_tiles=n_tiles, lead=lead),
        grid=(B, n_tiles + (1 if lead else 0)),
        in_specs=[
            pl.BlockSpec((None, tm, D), row),
            pl.BlockSpec((1, D), const2),
            pl.BlockSpec((D, 4 * SB_WIDTH), const2),
            pl.BlockSpec((1, SB_WIDTH), const2),
            pl.BlockSpec((1, SB_WIDTH), const2),
            pl.BlockSpec((SB_WIDTH, SB_WIDTH), const2),
            pl.BlockSpec((len(POOL_WINDOWS), POOL_GROUP, POOL_GROUP), lambda b, t: (0, 0, 0)),
            pl.BlockSpec((1, POOL_WIDTH), const2),
            pl.BlockSpec((None, CARRY_ROWS, POOL_WIDTH),
                         (lambda b, t: (b, 0, 0)) if per_batch_prev else (lambda b, t: (0, 0, 0))),
            pl.BlockSpec((None,) + k_lead.shape[1:], lambda b, t: (0, 0, 0)),
            pl.BlockSpec((None,) + v_lead.shape[1:], lambda b, t: (0, 0, 0)),
        ],
        out_specs=[blk(SB_WIDTH), blk(SB_WIDTH), blk(SB_WIDTH),
                   pl.BlockSpec((None, tm, SB_WIDTH), late), pl.BlockSpec((None, tm, SB_WIDTH), late), blk(POOL_WIDTH),
                   pl.BlockSpec((None, CARRY_ROWS, POOL_WIDTH), lambda b, t: (b, 0, 0))],
        out_shape=out_shape,
        scratch_shapes=[pltpu.VMEM((CARRY_ROWS, POOL_WIDTH), F32),
                        pltpu.VMEM(k_lead.shape[1:], F32), pltpu.VMEM(v_lead.shape[1:], F32)],
        compiler_params=_params("arbitrary", "arbitrary"),
        name="inproj",
    )(x, norm_mix, w_uqkv, qg, kg, hsum, pool_mix, pool_scale, uprev, k_lead, v_lead)


def _log_gates(z):
    sign = jnp.uint32(0x80000000)
    neg_abs = lax.bitcast_convert_type(lax.bitcast_convert_type(z, jnp.uint32) | sign, F32)
    sp = jnp.maximum(z, 0.0) + jnp.log(1.0 + jnp.exp(neg_abs))
    return sp, z - sp


def _dot_nt(a, b):
    return lax.dot_general(a, b, (((1,), (1,)), ((), ())), preferred_element_type=F32)


def _attn_body(vq_ref, vk_ref, q_ref, k_ref, v_ref, kp_ref, vp_ref, tri_ref, o_ref,
               z_scr, e_scr, acc_scr, run_scr, *, L, P, tq, n_visits, full_from_prefix, keys_on_lanes):
    nq = L // tq
    lane = lax.broadcasted_iota(jnp.int32, (1, HEAD_PAIR_LANES), 1)
    heads = (lane < SB_HEAD_DIM, lane >= SB_HEAD_DIM)
    ksrc, vsrc = (kp_ref, vp_ref) if full_from_prefix else (k_ref, v_ref)

    def stacked_q(qi):
        qp = q_ref[pl.ds(pl.multiple_of(qi * tq, tq), tq), :]
        zero = jnp.zeros((), BF16)
        return jnp.concatenate([jnp.where(heads[0], qp, zero), jnp.where(heads[1], qp, zero)], axis=0)

    def key_rows(s):
        return pl.ds(pl.multiple_of(vk_ref[s] * KEY_BLOCK, KEY_BLOCK), KEY_BLOCK)

    def stage_a(s):
        if keys_on_lanes:
            z_scr[...] = jnp.dot(stacked_q(vq_ref[s]), ksrc[:, key_rows(s)].astype(BF16), preferred_element_type=F32)
        else:
            z_scr[...] = _dot_nt(stacked_q(vq_ref[s]), ksrc[key_rows(s), :].astype(BF16))

    def stage_b(s):
        qi = vq_ref[s]
        sp, ls = _log_gates(z_scr[...])
        later = jnp.dot(sp.astype(BF16), tri_ref[...], preferred_element_type=F32)
        run = run_scr[qi]
        e_scr[...] = ls + (later + jnp.concatenate([run, run], axis=1))
        run_scr[qi] = run - jnp.sum(sp, axis=1, keepdims=True)

    def stage_c(s):
        qi = vq_ref[s]
        a = jnp.exp(e_scr[...]).astype(BF16)
        if keys_on_lanes:
            acc_scr[qi] = acc_scr[qi] + _dot_nt(a, vsrc[:, key_rows(s)].astype(BF16))
        else:
            acc_scr[qi] = acc_scr[qi] + jnp.dot(a, vsrc[key_rows(s), :].astype(BF16), preferred_element_type=F32)

    if n_visits:
        acc_scr[...] = jnp.zeros_like(acc_scr)
        run_scr[...] = jnp.zeros_like(run_scr)
        stage_a(0)
        if n_visits >= 2:
            stage_b(0)
            stage_a(1)

        def steady(s, c):
            stage_c(s - 2)
            stage_b(s - 1)
            stage_a(s)
            return c

        lax.fori_loop(2, n_visits, steady, 0)
        if n_visits >= 2:
            stage_c(n_visits - 2)
        stage_b(n_visits - 1)
        stage_c(n_visits - 1)

    rows = lax.broadcasted_iota(jnp.int32, (2 * tq, tq), 0)
    cols = lax.broadcasted_iota(jnp.int32, (2 * tq, tq), 1)
    visible = (cols < rows) & (rows < tq) | (cols < rows - tq)

    def q_block(qi):
        qrows = pl.ds(pl.multiple_of(qi * tq, tq), tq)
        q2 = stacked_q(qi)
        sp, ls = _log_gates(_dot_nt(q2, k_ref[qrows, :]))
        sp = jnp.where(visible, sp, 0.0)
        later = jnp.dot(sp.astype(BF16), tri_ref[0:tq, 0:tq], preferred_element_type=F32)
        a = jnp.where(visible, jnp.exp(ls + later), 0.0)
        acc = jnp.dot(a.astype(BF16), v_ref[qrows, :], preferred_element_type=F32)
        run = -jnp.sum(sp, axis=1, keepdims=True)
        if n_visits:
            acc = acc + jnp.exp(run) * acc_scr[qi]
            run = run + run_scr[qi][:, 0:1]
        if not full_from_prefix:
            sp, ls = _log_gates(_dot_nt(q2, kp_ref[...].astype(BF16)))
            later = jnp.dot(sp.astype(BF16), tri_ref[0:P, 0:P], preferred_element_type=F32)
            a = jnp.exp(ls + (later + run))
            acc = acc + jnp.dot(a.astype(BF16), vp_ref[...].astype(BF16), preferred_element_type=F32)
        o_ref[qrows, :] = jnp.where(heads[0], acc[:tq], acc[tq:]).astype(BF16)

    def q_block_group(j, carry):
        for u in range(PHASE2_GROUP):
            q_block(PHASE2_GROUP * j + u)
        return carry

    lax.fori_loop(0, nq // PHASE2_GROUP, q_block_group, 0)
    for qi in range(nq - nq % PHASE2_GROUP, nq):
        q_block(qi)


def _attention(q, k, v, kp, vp, tri, keys_on_lanes=False):
    B, L, _ = q.shape
    P = kp.shape[2] if keys_on_lanes else kp.shape[1]
    tq = min(KEY_BLOCK, L)
    nq = L // tq
    assert L % tq == 0 and tq % 16 == 0 and P % 16 == 0
    full_from_prefix = P > KEY_BLOCK
    if full_from_prefix:
        assert nq == 1 and P % KEY_BLOCK == 0
        visits = [(qi, kb) for qi in range(nq) for kb in reversed(range(P // KEY_BLOCK))]
    else:
        assert tq == KEY_BLOCK or nq == 1
        visits = [(qi, kb) for qi in range(nq) for kb in reversed(range(qi))]
    n_visits = len(visits)
    vq = jnp.asarray([qi for qi, _ in visits] or [0], jnp.int32)
    vk = jnp.asarray([kb for _, kb in visits] or [0], jnp.int32)
    per_batch_prefix = kp.shape[0] == B and B > 1
    seq = pl.BlockSpec((None, L, HEAD_PAIR_LANES), lambda b, hp, *_: (b, 0, hp))
    if keys_on_lanes:
        assert full_from_prefix and per_batch_prefix
        pre = pl.BlockSpec((None, HEAD_PAIR_LANES, P), lambda b, hp, *_: (b, hp, 0))
    else:
        pre = pl.BlockSpec((None, P, HEAD_PAIR_LANES),
                           (lambda b, hp, *_: (b, 0, hp)) if per_batch_prefix else (lambda b, hp, *_: (0, 0, hp)))
    grid_spec = pltpu.PrefetchScalarGridSpec(
        num_scalar_prefetch=2,
        grid=(B, SB_WIDTH // HEAD_PAIR_LANES),
        in_specs=[seq, seq, seq, pre, pre, pl.BlockSpec((KEY_BLOCK, KEY_BLOCK), lambda b, hp, *_: (0, 0))],
        out_specs=seq,
        scratch_shapes=[pltpu.VMEM((2 * tq, KEY_BLOCK), F32), pltpu.VMEM((2 * tq, KEY_BLOCK), F32),
                        pltpu.VMEM((nq, 2 * tq, HEAD_PAIR_LANES), F32),
                        pltpu.VMEM((nq, 2 * tq, HEAD_PAIR_LANES), F32)],
    )
    return pl.pallas_call(
        functools.partial(_attn_body, L=L, P=P, tq=tq, n_visits=n_visits, full_from_prefix=full_from_prefix,
                          keys_on_lanes=keys_on_lanes),
        grid_spec=grid_spec,
        out_shape=jax.ShapeDtypeStruct((B, L, SB_WIDTH), BF16),
        compiler_params=_params("arbitrary", "arbitrary"),
        name="stickbreak",
    )(vq, vk, q, k, v, kp, vp, tri)


def _transpose8(v):
    sub = lax.broadcasted_iota(jnp.int32, (1, ROW_TILE[0], 1), 1)
    for k in (4, 2, 1):
        low = (sub & k) == 0
        nxt = list(v)
        for i in range(ROW_TILE[0]):
            if i & k == 0:
                j = i | k
                nxt[i] = jnp.where(low, v[i], pltpu.roll(v[j], k, axis=1))
                nxt[j] = jnp.where(low, pltpu.roll(v[i], ROW_TILE[0] - k, axis=1), v[j])
        v = nxt
    return v


def _store_rows(ref, val):
    s, w = ROW_TILE
    g = val.shape[0] // s
    v = _transpose8([val[:, c * w:(c + 1) * w].reshape(g, s, w) for c in range(s)])
    for r in range(s):
        ref[pl.ds(r, g, stride=s), :, :] = v[r]


def _load_rows(ref):
    s, w = ROW_TILE
    g = ref.shape[0] // s
    v = _transpose8([ref[pl.ds(r, g, stride=s), :, :] for r in range(s)])
    return jnp.concatenate([x.reshape(g * s, w) for x in v], axis=1)


def _merge_body(xp_ref, pop_ref, sbp_ref, xs_ref, pos_ref, sbs_ref,
                nm_ref, wg_ref, wpo_ref, wsb_ref, wo_ref, nf_ref, wr_ref, br_ref, low_ref,
                h2_ref, hn2_ref, ri_ref, rw_ref, cnt_ref, carry_ref, *, prompt_tiles):
    @pl.when(pl.program_id(0) == 0)
    def _():
        carry_ref[...] = jnp.zeros_like(carry_ref)

    from_prompt = pl.program_id(0) < prompt_tiles
    x = jnp.where(from_prompt, xp_ref[...], xs_ref[...])
    po = jnp.where(from_prompt, pop_ref[...], pos_ref[...])
    sb = jnp.where(from_prompt, sbp_ref[...], sbs_ref[...])
    D = x.shape[1]
    hn = _rms(x, nm_ref[...]).astype(BF16)
    gates = jnp.dot(hn, wg_ref[...], preferred_element_type=F32)
    pool_proj = jnp.dot(po, wpo_ref[...], preferred_element_type=F32)
    sb_proj = jnp.dot(sb, wsb_ref[...], preferred_element_type=F32)
    merged = jax.nn.sigmoid(gates[:, :D]) * pool_proj + jax.nn.sigmoid(gates[:, D:]) * sb_proj
    h2 = x + jnp.dot(merged.astype(BF16), wo_ref[...], preferred_element_type=F32)
    h2_ref[...] = h2
    hn2 = _rms(h2, nf_ref[...])
    _store_rows(hn2_ref, hn2)

    logits = jnp.dot(hn2.astype(BF16), wr_ref[...], preferred_element_type=F32) + br_ref[...]
    lane = lax.broadcasted_iota(jnp.int32, logits.shape, 1)
    lanef = lane.astype(F32)
    neg = jnp.float32(-jnp.inf)
    big = jnp.float32(2 * ROUTER_LANES)

    def first_argmax(vals):
        m = jnp.max(vals, axis=1, keepdims=True)
        return m, jnp.min(jnp.where(vals == m, lanef, big), axis=1, keepdims=True)

    is_group = lane < N_GROUPS
    gmax, gidx = first_argmax(jnp.where(is_group, logits, neg))
    p_group = 1.0 / jnp.sum(jnp.where(is_group, jnp.exp(logits - gmax), 0.0), axis=1, keepdims=True)
    first = N_GROUPS + EXPERTS_PER_GROUP * gidx
    in_group = (lanef >= first) & (lanef < first + EXPERTS_PER_GROUP)
    cand = jnp.where(in_group, logits, neg)
    m1, i1 = first_argmax(cand)
    m2, i2 = first_argmax(jnp.where(lanef == i1, neg, cand))
    t = jnp.exp(m2 - m1)
    w1 = p_group / (1.0 + t)
    w2 = p_group * t / (1.0 + t)

    hit1 = lanef == i1
    hit2 = lanef == i2
    onehot = jnp.where(hit1 | hit2, 1.0, 0.0)
    before = jnp.dot(low_ref[...], onehot.astype(BF16), preferred_element_type=F32) + carry_ref[...]
    r1 = jnp.sum(jnp.where(hit1, before, 0.0), axis=1, keepdims=True)
    r2 = jnp.sum(jnp.where(hit2, before, 0.0), axis=1, keepdims=True)
    carry_ref[...] = carry_ref[...] + jnp.sum(onehot, axis=0, keepdims=True)
    cnt_ref[...] = carry_ref[...]

    slab = jnp.where(lane == 0, i1 - N_GROUPS,
                     jnp.where(lane == 1, i2 - N_GROUPS, jnp.where(lane == 2, r1, jnp.where(lane == 3, r2, 0.0))))
    ri_ref[...] = slab.astype(jnp.int32)
    rw_ref[...] = jnp.where(lane == 0, w1, jnp.where(lane == 1, w2, 0.0))


def _merge_route(prompt, sample, norm_mix, w_gates, w_pool_out, w_sb_out, w_o, norm_ffn, w_router, b_router, lower):
    tm = lower.shape[0]
    (xp, pop, sbp), (xs, pos, sbs) = prompt, sample
    D = xp.shape[1]
    assert xp.shape[0] % tm == 0 and xs.shape[0] % tm == 0 and D == ROW_TILE[0] * ROW_TILE[1]
    tp, ts = xp.shape[0] // tm, xs.shape[0] // tm
    N = xp.shape[0] + xs.shape[0]
    from_p = lambda i: (jnp.minimum(i, tp - 1), 0)
    from_s = lambda i: (jnp.maximum(i - tp, 0), 0)
    row = lambda i: (i, 0)
    const = lambda i: (0, 0)
    full = lambda a: pl.BlockSpec(a.shape, const)
    group = lambda m: [pl.BlockSpec((tm, D), m), pl.BlockSpec((tm, POOL_WIDTH), m), pl.BlockSpec((tm, SB_WIDTH), m)]
    return pl.pallas_call(
        functools.partial(_merge_body, prompt_tiles=tp),
        grid=(tp + ts,),
        in_specs=group(from_p) + group(from_s) + [
            full(norm_mix), full(w_gates), full(w_pool_out), full(w_sb_out), full(w_o), full(norm_ffn),
            full(w_router), full(b_router), full(lower)],
        out_specs=[pl.BlockSpec((tm, D), row), pl.BlockSpec((tm,) + ROW_TILE, lambda i: (i, 0, 0)),
                   pl.BlockSpec((tm, ROUTER_LANES), row), pl.BlockSpec((tm, ROUTER_LANES), row),
                   pl.BlockSpec((1, ROUTER_LANES), const)],
        out_shape=(jax.ShapeDtypeStruct((N, D), F32), jax.ShapeDtypeStruct((N,) + ROW_TILE, F32),
                   jax.ShapeDtypeStruct((N, ROUTER_LANES), jnp.int32), jax.ShapeDtypeStruct((N, ROUTER_LANES), F32),
                   jax.ShapeDtypeStruct((1, ROUTER_LANES), F32)),
        scratch_shapes=[pltpu.VMEM((1, ROUTER_LANES), F32)],
        compiler_params=_params("arbitrary"),
        name="merge_route",
    )(xp, pop, sbp, xs, pos, sbs, norm_mix, w_gates, w_pool_out, w_sb_out, w_o, norm_ffn, w_router, b_router, lower)


def _row_copy(src_ref, src_row, dst_ref, dst_row, sem):
    return pltpu.make_async_copy(src_ref.at[src_row], dst_ref.at[dst_row], sem)


def _slots(pstart_ref, route_ref, r):
    return (pstart_ref[route_ref[0, 4 * r]] + route_ref[0, 4 * r + 2],
            pstart_ref[route_ref[0, 4 * r + 1]] + route_ref[0, 4 * r + 3])


def _dispatch_body(pend_ref, padded_ref, pstart_ref, route_ref, x_ref, buf_ref, slot_ref, zero_ref, sem, zsem,
                   *, tm, bm):
    @pl.when(pl.program_id(0) == 0)
    def _():
        zero_ref[...] = jnp.zeros_like(zero_ref)
        for e in range(N_EXPERTS):
            @pl.when(padded_ref[e] > 0)
            def _():
                cp = pltpu.make_async_copy(zero_ref, buf_ref.at[pl.ds(pend_ref[e] - bm, bm)], zsem)
                cp.start()
                cp.wait()

        def zero_unused_block(bi, c):
            cp = pltpu.make_async_copy(zero_ref, buf_ref.at[pl.ds(bi * bm, bm)], zsem)
            cp.start()
            cp.wait()
            return c

        lax.fori_loop(lax.div(pend_ref[N_EXPERTS - 1], bm), buf_ref.shape[0] // bm, zero_unused_block, 0)

    def issue(j, c):
        for u in range(ROW_DMA_UNROLL):
            r = j * ROW_DMA_UNROLL + u
            d0, d1 = _slots(pstart_ref, route_ref, r)
            slot_ref[0, 2 * r] = d0
            slot_ref[0, 2 * r + 1] = d1
            _row_copy(x_ref, r, buf_ref, d0, sem).start(priority=0)
            _row_copy(x_ref, r, buf_ref, d1, sem).start(priority=1)
        return c

    lax.fori_loop(0, tm // ROW_DMA_UNROLL, issue, 0)

    def drain(j, c):
        for _ in range(2 * ROW_DMA_UNROLL):
            _row_copy(x_ref, 0, buf_ref, 0, sem).wait()
        return c

    lax.fori_loop(0, tm // ROW_DMA_UNROLL, drain, 0)


def _dispatch(hn2, route, pend, padded, pstart, cap, bm, tm):
    N, C, W = hn2.shape
    grid_spec = pltpu.PrefetchScalarGridSpec(
        num_scalar_prefetch=3,
        grid=(N // tm,),
        in_specs=[pl.BlockSpec((None, 1, 4 * tm), lambda i, *_: (i, 0, 0), memory_space=pltpu.SMEM),
                  pl.BlockSpec((tm, C, W), lambda i, *_: (i, 0, 0))],
        out_specs=[pl.BlockSpec(memory_space=pl.ANY),
                   pl.BlockSpec((None, 1, 2 * tm), lambda i, *_: (i, 0, 0), memory_space=pltpu.SMEM)],
        scratch_shapes=[pltpu.VMEM((bm, C, W), F32), pltpu.SemaphoreType.DMA, pltpu.SemaphoreType.DMA],
    )
    return pl.pallas_call(
        functools.partial(_dispatch_body, tm=tm, bm=bm),
        grid_spec=grid_spec,
        out_shape=(jax.ShapeDtypeStruct((cap, C, W), F32), jax.ShapeDtypeStruct((N // tm, 1, 2 * tm), jnp.int32)),
        compiler_params=_params("arbitrary"),
        name="dispatch",
    )(pend, padded, pstart, route, hn2)


def _expert_body(blk_e_ref, nact_ref, x_ref, wg_ref, wu_ref, wd_ref, y_ref, wg_bf, wu_bf, wd_bf):
    i = pl.program_id(0)

    @pl.when(i < nact_ref[0])
    def _():
        @pl.when((i == 0) | (blk_e_ref[i] != blk_e_ref[jnp.maximum(i - 1, 0)]))
        def _():
            wg_bf[...] = wg_ref[...].astype(BF16)
            wu_bf[...] = wu_ref[...].astype(BF16)
            wd_bf[...] = wd_ref[...].astype(BF16)

        x = _load_rows(x_ref).astype(BF16)
        gate = jnp.dot(x, wg_bf[...], preferred_element_type=F32)
        up = jnp.dot(x, wu_bf[...], preferred_element_type=F32)
        hidden = (jax.nn.silu(gate) * up).astype(BF16)
        _store_rows(y_ref, jnp.dot(hidden, wd_bf[...], preferred_element_type=F32))

    @pl.when(pl.program_id(0) >= nact_ref[0])
    def _():
        y_ref[...] = jnp.zeros_like(y_ref)


def _experts(buf, blk_e, nact, w_gate, w_up, w_down, bm):
    cap, C, W = buf.shape
    D, DE = w_gate.shape[1], w_gate.shape[2]
    rows = lambda i, be, na: (jnp.minimum(i, na[0] - 1), 0, 0)
    grid_spec = pltpu.PrefetchScalarGridSpec(
        num_scalar_prefetch=2,
        grid=(cap // bm,),
        in_specs=[pl.BlockSpec((bm, C, W), rows),
                  pl.BlockSpec((None, D, DE), lambda i, be, na: (be[i], 0, 0)),
                  pl.BlockSpec((None, D, DE), lambda i, be, na: (be[i], 0, 0)),
                  pl.BlockSpec((None, DE, D), lambda i, be, na: (be[i], 0, 0))],
        out_specs=pl.BlockSpec((bm, C, W), lambda i, be, na: (i, 0, 0)),
        scratch_shapes=[pltpu.VMEM((D, DE), BF16), pltpu.VMEM((D, DE), BF16), pltpu.VMEM((DE, D), BF16)],
    )
    return pl.pallas_call(
        _expert_body,
        grid_spec=grid_spec,
        out_shape=jax.ShapeDtypeStruct((cap, C, W), F32),
        compiler_params=_params("arbitrary"),
        name="experts",
    )(blk_e, nact, buf, w_gate, w_up, w_down)


def _combine_body(pstart_ref, route_ref, nxt_ref, h2_ref, rw_ref, yb_ref, yp_ref, ys_ref, rows_ref, sems,
                  *, tm, prompt_tiles, n_tiles):
    i = pl.program_id(0)
    slot = i % 2

    def fetch(rt_ref, into):
        def issue(j, c):
            for u in range(ROW_DMA_UNROLL):
                r = j * ROW_DMA_UNROLL + u
                _row_copy(yb_ref, rt_ref[0, 2 * r], rows_ref.at[into, 0], r, sems.at[into]).start(priority=0)
                _row_copy(yb_ref, rt_ref[0, 2 * r + 1], rows_ref.at[into, 1], r, sems.at[into]).start(priority=1)
            return c

        lax.fori_loop(0, tm // ROW_DMA_UNROLL, issue, 0)

    @pl.when(i == 0)
    def _():
        fetch(route_ref, 0)

    @pl.when(i + 1 < n_tiles)
    def _():
        fetch(nxt_ref, 1 - slot)

    def drain(j, c):
        for _ in range(2 * ROW_DMA_UNROLL):
            _row_copy(yb_ref, 0, rows_ref.at[slot, 0], 0, sems.at[slot]).wait()
        return c

    lax.fori_loop(0, tm // ROW_DMA_UNROLL, drain, 0)
    w = rw_ref[...]
    y = h2_ref[...] + (_load_rows(rows_ref.at[slot, 0]) * w[:, 0:1] + _load_rows(rows_ref.at[slot, 1]) * w[:, 1:2])

    @pl.when(i < prompt_tiles)
    def _():
        yp_ref[...] = y

    @pl.when(i >= prompt_tiles)
    def _():
        ys_ref[...] = y


def _combine(h2, rw, route, pstart, yb, tm, n_prompt):
    N, D = h2.shape
    C, W = yb.shape[1:]
    n_tiles, tp = N // tm, n_prompt // tm
    idx = lambda off: pl.BlockSpec((None, 1, 2 * tm), lambda i, ps: (jnp.minimum(i + off, n_tiles - 1), 0, 0),
                                   memory_space=pltpu.SMEM)
    grid_spec = pltpu.PrefetchScalarGridSpec(
        num_scalar_prefetch=1,
        grid=(n_tiles,),
        in_specs=[idx(0), idx(1),
                  pl.BlockSpec((tm, D), lambda i, ps: (i, 0)),
                  pl.BlockSpec((tm, ROUTER_LANES), lambda i, ps: (i, 0)),
                  pl.BlockSpec(memory_space=pl.ANY)],
        out_specs=[pl.BlockSpec((tm, D), lambda i, ps: (jnp.minimum(i, tp - 1), 0)),
                   pl.BlockSpec((tm, D), lambda i, ps: (jnp.maximum(i - tp, 0), 0))],
        scratch_shapes=[pltpu.VMEM((2, 2, tm, C, W), F32), pltpu.SemaphoreType.DMA((2,))],
    )
    return pl.pallas_call(
        functools.partial(_combine_body, tm=tm, prompt_tiles=tp, n_tiles=n_tiles),
        grid_spec=grid_spec,
        out_shape=(jax.ShapeDtypeStruct((n_prompt, D), F32), jax.ShapeDtypeStruct((N - n_prompt, D), F32)),
        compiler_params=_params("arbitrary"),
        name="combine",
    )(pstart, route, route, h2, rw, yb)


def _moe(h2, hn2, ri, rw, counts, w_gate, w_up, w_down, tm, n_prompt, bm):
    N, D = h2.shape
    cnt = counts[0, N_GROUPS:N_GROUPS + N_EXPERTS].astype(jnp.int32)
    padded = (cnt + bm - 1) // bm * bm
    pend = jnp.cumsum(padded).astype(jnp.int32)
    pstart = pend - padded
    route = ri[:, 0:4].reshape(N // tm, 1, 4 * tm)
    cap = -(-(2 * N + N_EXPERTS * (bm - 1)) // bm) * bm
    nblk = cap // bm
    starts = jnp.arange(nblk, dtype=jnp.int32)[:, None] * bm
    blk_e = jnp.minimum(jnp.sum((pend[None, :] <= starts).astype(jnp.int32), axis=1), N_EXPERTS - 1)
    nact = pend[-1:] // bm
    buf, slots = _dispatch(hn2, route, pend, padded, pstart, cap, bm, tm)
    yb = _experts(buf, blk_e, nact, w_gate, w_up, w_down, bm)
    return _combine(h2, rw, slots, pstart, yb, tm, n_prompt)


def _constants(tm):
    r = lax.broadcasted_iota(jnp.int32, (KEY_BLOCK, KEY_BLOCK), 0)
    c = lax.broadcasted_iota(jnp.int32, (KEY_BLOCK, KEY_BLOCK), 1)
    tri = jnp.where(r > c, -1.0, 0.0).astype(BF16)
    hr = lax.broadcasted_iota(jnp.int32, (SB_WIDTH, SB_WIDTH), 0) // SB_HEAD_DIM
    hc = lax.broadcasted_iota(jnp.int32, (SB_WIDTH, SB_WIDTH), 1) // SB_HEAD_DIM
    hsum = jnp.where(hr == hc, 1.0 / SB_HEAD_DIM, 0.0).astype(BF16)
    lr = lax.broadcasted_iota(jnp.int32, (tm, tm), 0)
    lc = lax.broadcasted_iota(jnp.int32, (tm, tm), 1)
    lower = jnp.where(lc < lr, 1.0, 0.0).astype(BF16)
    return tri, hsum, lower


def kernel(x_prompt, x_sample, cache_sb_k, cache_sb_v, state_pool, meta_tokens, norm_mix, w_in, q_norm, k_norm,
           pool_mix, pool_scale, w_pool_out, w_sb_out, w_o, norm_ffn, w_router_group, b_router_group,
           w_router_expert, b_router_expert, w_gate, w_up, w_down):
    assert norm_mix.shape[0] == 1, "single-layer trunk"
    B, S, D = x_prompt.shape
    BS, SS, _ = x_sample.shape
    past = cache_sb_k.shape[2]
    tm = min(512, B * S, BS * SS)
    tri, hsum, lower = _constants(tm)

    w_in0 = w_in[0]
    w_uqkv = w_in0[:, :4 * SB_WIDTH].astype(BF16)
    w_gates = w_in0[:, 4 * SB_WIDTH:].astype(BF16)
    nm = norm_mix[0][None, :]
    nf = norm_ffn[0][None, :]
    qg = jnp.tile(q_norm[0], SB_HEADS)[None, :]
    kg = jnp.tile(k_norm[0], SB_HEADS)[None, :]
    pm = pool_mix[0].astype(BF16)
    ps = pool_scale[0][None, :]
    wpo = w_pool_out[0].astype(BF16)
    wsb = w_sb_out[0].astype(BF16)
    wo = w_o[0].astype(BF16)
    pad = ROUTER_LANES - N_GROUPS - N_EXPERTS
    w_router = jnp.concatenate([w_router_group[0], w_router_expert[0], jnp.zeros((D, pad), F32)], axis=1).astype(BF16)
    b_router = jnp.concatenate([b_router_group[0], b_router_expert[0], jnp.zeros((pad,), F32)])[None, :]

    inproj = functools.partial(_inproj, norm_mix=nm, w_uqkv=w_uqkv, qg=qg, kg=kg, hsum=hsum, pool_mix=pm,
                               pool_scale=ps)

    shape5 = lambda a: a.reshape(1, a.shape[0], a.shape[1], SB_HEADS, SB_HEAD_DIM)
    flat = lambda a: a.reshape(a.shape[0] * a.shape[1], a.shape[2])

    prev_s = jnp.concatenate([jnp.zeros((BS, CARRY_ROWS - POOL_STATE, POOL_WIDTH), F32), state_pool[0]], axis=1)
    q_s, k_s, v_s, ks_f, vs_f, po_s, utail_s = inproj(x_sample, prev_s)
    kc = jnp.transpose(cache_sb_k[0], (0, 2, 3, 1)).reshape(BS, SB_WIDTH, past)
    vc = jnp.transpose(cache_sb_v[0], (0, 2, 3, 1)).reshape(BS, SB_WIDTH, past)
    sb_s = _attention(q_s, k_s, v_s, kc, vc, tri, keys_on_lanes=True)
    new_pool_sample = utail_s[None, :, CARRY_ROWS - POOL_STATE:, :]

    zeros_prev = jnp.zeros((1, CARRY_ROWS, POOL_WIDTH), F32)
    _, km_b, vm_b, km_f, vm_f, _, u_meta = inproj(meta_tokens[None], zeros_prev)

    q_b, k_b, v_b, k_f, v_f, po, utail = inproj(x_prompt, u_meta, k_lead=km_f, v_lead=vm_f)
    sb = _attention(q_b, k_b, v_b, km_b, vm_b, tri)
    new_k_prompt, new_v_prompt = shape5(k_f), shape5(v_f)
    new_pool_prompt = utail[None, :, CARRY_ROWS - POOL_STATE:, :]

    h2, hn2, ri, rw, counts = _merge_route((flat(x_prompt), flat(po), flat(sb)), (flat(x_sample), flat(po_s), flat(sb_s)),
                                           nm, w_gates, wpo, wsb, wo, nf, w_router, b_router, lower)
    y_p, y_s = _moe(h2, hn2, ri, rw, counts, w_gate[0], w_up[0], w_down[0], tm, B * S, MOE_BLOCK_ROWS)
    y_prompt, y_sample = y_p.reshape(x_prompt.shape), y_s.reshape(x_sample.shape)

    return (y_prompt, y_sample, new_k_prompt, new_v_prompt, new_pool_prompt,
            shape5(ks_f), shape5(vs_f), new_pool_sample)
```
